```python
import jax, jax.numpy as jnp
from jax import lax
import numpy as np

D_MODEL = 1024
BATCH = 4
SEQ = 4096
DEPTH = 1
DEC_BATCH = 4
DEC_SEQ = 8192
PAST_LEN = 128

D_INNER = 2 * D_MODEL
D_SSD = D_INNER // 2
D_ATTN = D_INNER - D_SSD
HEAD_DIM = 64
N_ATTN_HEADS = D_ATTN // HEAD_DIM
SSD_HEAD_DIM = 64
N_SSD_HEADS = D_SSD // SSD_HEAD_DIM
SSD_GROUPS = 2
D_STATE = 128
CONV_WIDTH = 5
CHUNK = 128
ROPE_THETA = 10000.0
EPS = 1e-5
DT_MIN = 0.001
DT_MAX = 0.1
DILATED_PATTERNS = ((128, 1), (512, 4), (2048, 16))
D_XBC = D_SSD + 2 * SSD_GROUPS * D_STATE
PROJ_SIZES = (D_ATTN, D_ATTN, D_ATTN, D_ATTN, D_SSD, D_XBC, N_SSD_HEADS, N_SSD_HEADS)
D_PROJ = 4 * D_ATTN + D_SSD + D_XBC + 2 * N_SSD_HEADS

kernel_name = 'hybrid_ssd_dilated_attn_encoder'


def rms_norm(x, g):
    xf = x.astype(jnp.float32)
    xf = xf * lax.rsqrt(jnp.mean(xf * xf, axis=-1, keepdims=True) + EPS)
    return (xf * g.astype(jnp.float32)).astype(x.dtype)


def rotary(t):
    s_len, dh = t.shape[1], t.shape[-1]
    half = dh // 2
    inv = ROPE_THETA ** (-jnp.arange(half, dtype=jnp.float32) * 2.0 / dh)
    ang = jnp.arange(s_len, dtype=jnp.float32)[:, None] * inv[None, :]
    cos = jnp.cos(ang)[None, :, None, :]
    sin = jnp.sin(ang)[None, :, None, :]
    t1, t2 = t[..., :half], t[..., half:]
    return jnp.concatenate([t1 * cos - t2 * sin, t2 * cos + t1 * sin], axis=-1)


def dilated_window_attention(q, k, v, window, dilation):
    b, s_len, h, dh = q.shape
    r = window // (2 * dilation)
    l_sub = s_len // dilation
    n = b * dilation

    def to_sub(t):
        return t.reshape(b, l_sub, dilation, h, dh).transpose(0, 2, 1, 3, 4).reshape(n, l_sub, h, dh)

    qs, ks, vs = to_sub(q), to_sub(k), to_sub(v)
    nb = -(-l_sub // r)
    lp = nb * r
    qb = jnp.pad(qs, ((0, 0), (0, lp - l_sub), (0, 0), (0, 0))).reshape(n, nb, r, h, dh)
    kpad = ((0, 0), (r, lp - l_sub + r), (0, 0), (0, 0))
    kp = jnp.pad(ks, kpad).reshape(n, nb + 2, r, h, dh)
    vp = jnp.pad(vs, kpad).reshape(n, nb + 2, r, h, dh)
    kw = jnp.concatenate([kp[:, :-2], kp[:, 1:-1], kp[:, 2:]], axis=2)
    vw = jnp.concatenate([vp[:, :-2], vp[:, 1:-1], vp[:, 2:]], axis=2)
    scores = jnp.einsum('nbqhd,nbkhd->nbhqk', qb, kw) * (HEAD_DIM ** -0.5)
    qi = jnp.arange(nb)[:, None] * r + jnp.arange(r)[None, :]
    ki = jnp.arange(nb)[:, None] * r - r + jnp.arange(3 * r)[None, :]
    valid = ((jnp.abs(qi[:, :, None] - ki[:, None, :]) <= r)
             & (ki[:, None, :] >= 0) & (ki[:, None, :] < l_sub))
    scores = jnp.where(valid[None, :, None], scores, -jnp.inf)
    lse = jax.nn.logsumexp(scores, axis=-1)
    p = jnp.exp(scores - lse[..., None])
    o = jnp.einsum('nbhqk,nbkhd->nbqhd', p, vw)
    o = o.reshape(n, lp, h, dh)[:, :l_sub]
    lse = lse.transpose(0, 1, 3, 2).reshape(n, lp, h)[:, :l_sub]
    o = o.reshape(b, dilation, l_sub, h, dh).transpose(0, 2, 1, 3, 4).reshape(b, s_len, h, dh)
    lse = lse.reshape(b, dilation, l_sub, h).transpose(0, 2, 1, 3).reshape(b, s_len, h)
    return o, lse


def dilated_mixture(q, k, v):
    outs, lses = [], []
    for window, dilation in DILATED_PATTERNS:
        o, lse = dilated_window_attention(q, k, v, window, dilation)
        outs.append(o)
        lses.append(lse)
    w = jax.nn.softmax(jnp.stack(lses, axis=0), axis=0)
    return jnp.sum(w[..., None] * jnp.stack(outs, axis=0), axis=0)


def depthwise_conv_centered(u, w, bias):
    pad = (CONV_WIDTH - 1) // 2
    out = lax.conv_general_dilated(u, w[:, None, :], window_strides=(1,), padding=[(pad, pad)],
                                   dimension_numbers=('NWC', 'WIO', 'NWC'),
                                   feature_group_count=u.shape[-1])
    return out + bias


def ssd_scan(x, dt_raw, b_mat, c_mat, a_log, dt_bias):
    bsz, s_len, h, p = x.shape
    g, nst = b_mat.shape[2], b_mat.shape[3]
    e = h // g
    nc = s_len // CHUNK
    dt = jax.nn.softplus(dt_raw + dt_bias)
    a = dt * (-jnp.exp(a_log))
    xc = (x * dt[..., None]).reshape(bsz, nc, CHUNK, g, e, p)
    ac = a.reshape(bsz, nc, CHUNK, g, e)
    bc = b_mat.reshape(bsz, nc, CHUNK, g, nst)
    cc = c_mat.reshape(bsz, nc, CHUNK, g, nst)
    acum = jnp.cumsum(ac, axis=2)
    seg = acum[:, :, :, None] - acum[:, :, None, :]
    tril = jnp.tril(jnp.ones((CHUNK, CHUNK), dtype=bool))[:, :, None, None]
    lmat = jnp.exp(jnp.where(tril, seg, -jnp.inf))
    cb = jnp.einsum('bclgn,bcsgn->bclsg', cc, bc)
    y_diag = jnp.einsum('bclsge,bcsgep->bclgep', cb[..., None] * lmat, xc)
    decay_to_end = jnp.exp(acum[:, :, -1:] - acum)
    states = jnp.einsum('bcsgn,bcsgep->bcgepn', bc, decay_to_end[..., None] * xc)
    chunk_decay = jnp.exp(acum[:, :, -1])

    def step(carry, inp):
        st, dec = inp
        return carry * dec[..., None, None] + st, carry

    init = jnp.zeros((bsz, g, e, p, nst), dtype=x.dtype)
    _, states_in = lax.scan(step, init, (jnp.moveaxis(states, 1, 0), jnp.moveaxis(chunk_decay, 1, 0)))
    states_in = jnp.moveaxis(states_in, 0, 1)
    y_off = jnp.einsum('bclgn,bcgepn->bclgep', cc, states_in) * jnp.exp(acum)[..., None]
    return (y_diag + y_off).reshape(bsz, s_len, h, p)


def ssd_branch(z, xbc, dt_f, dt_b, conv_w, conv_b, dt_bias_fwd, dt_bias_bwd,
               a_log_fwd, a_log_bwd, d_skip, norm_ssd):
    bsz, s_len, _ = z.shape
    f32 = jnp.float32
    xbc = jax.nn.silu(depthwise_conv_centered(xbc.astype(f32), conv_w.astype(f32), conv_b.astype(f32)))
    xs = xbc[..., :D_SSD].reshape(bsz, s_len, N_SSD_HEADS, SSD_HEAD_DIM)
    b_mat = xbc[..., D_SSD:D_SSD + SSD_GROUPS * D_STATE].reshape(bsz, s_len, SSD_GROUPS, D_STATE)
    c_mat = xbc[..., D_SSD + SSD_GROUPS * D_STATE:].reshape(bsz, s_len, SSD_GROUPS, D_STATE)
    y_fwd = ssd_scan(xs, dt_f.astype(f32), b_mat, c_mat, a_log_fwd.astype(f32), dt_bias_fwd.astype(f32))
    fl = lambda t: jnp.flip(t, axis=1)
    y_bwd = fl(ssd_scan(fl(xs), fl(dt_b.astype(f32)), fl(b_mat), fl(c_mat),
                        a_log_bwd.astype(f32), dt_bias_bwd.astype(f32)))
    y = y_fwd + y_bwd + d_skip.astype(f32)[:, None] * xs
    y = y.reshape(bsz, s_len, D_SSD) * jax.nn.silu(z.astype(f32))
    return rms_norm(y, norm_ssd)


def attention_branch(q, k, v, z):
    bsz, s_len, _ = q.shape
    f32 = jnp.float32
    shp = (bsz, s_len, N_ATTN_HEADS, HEAD_DIM)
    qh = rotary(q.astype(f32).reshape(shp))
    kh = rotary(k.astype(f32).reshape(shp))
    vh = v.astype(f32).reshape(shp)
    o = dilated_mixture(qh, kh, vh).reshape(bsz, s_len, D_ATTN)
    return o * jax.nn.silu(z.astype(f32))


def encoder(x, w_in, conv_w, conv_b, dt_bias_fwd, dt_bias_bwd, a_log_fwd, a_log_bwd,
            d_skip, norm_ssd, norm_in, w_out, norm_f):
    split_at = [int(i) for i in np.cumsum(PROJ_SIZES)[:-1]]
    for layer in range(DEPTH):
        h = rms_norm(x, norm_in[layer])
        proj = h @ w_in[layer]
        q, k, v, z_attn, z_ssd, xbc, dt_f, dt_b = jnp.split(proj, split_at, axis=-1)
        y_ssd = ssd_branch(z_ssd, xbc, dt_f, dt_b, conv_w[layer], conv_b[layer],
                           dt_bias_fwd[layer], dt_bias_bwd[layer], a_log_fwd[layer],
                           a_log_bwd[layer], d_skip[layer], norm_ssd[layer])
        y_attn = attention_branch(q, k, v, z_attn)
        mix = jnp.concatenate([y_ssd.astype(x.dtype), y_attn.astype(x.dtype)], axis=-1)
        x = x + mix @ w_out[layer]
    return rms_norm(x, norm_f)


def setup_inputs(seed: int = 0) -> dict:
    key = jax.random.key(seed)
    ks = jax.random.split(key, 16)
    f32 = jnp.float32
    x_prompt = jax.random.normal(ks[0], (BATCH, SEQ, D_MODEL), f32)
    x_sample = jax.random.normal(ks[1], (DEC_BATCH, DEC_SEQ, D_MODEL), f32)
    w_in = jax.random.normal(ks[2], (DEPTH, D_MODEL, D_PROJ), f32) * D_MODEL ** -0.5
    conv_w = jax.random.normal(ks[3], (DEPTH, CONV_WIDTH, D_XBC), f32) * CONV_WIDTH ** -0.5
    conv_b = jax.random.normal(ks[4], (DEPTH, D_XBC), f32) * 0.02

    def dt_bias_init(k):
        u = jax.random.uniform(k, (DEPTH, N_SSD_HEADS), f32)
        dt = jnp.exp(u * (jnp.log(DT_MAX) - jnp.log(DT_MIN)) + jnp.log(DT_MIN))
        return dt + jnp.log(-jnp.expm1(-dt))

    dt_bias_fwd = dt_bias_init(ks[5])
    dt_bias_bwd = dt_bias_init(ks[6])
    a_log_fwd = jnp.log(jax.random.uniform(ks[7], (DEPTH, N_SSD_HEADS), f32, minval=1.0, maxval=16.0))
    a_log_bwd = jnp.log(jax.random.uniform(ks[8], (DEPTH, N_SSD_HEADS), f32, minval=1.0, maxval=16.0))
    d_skip = 1.0 + 0.1 * jax.random.normal(ks[9], (DEPTH, N_SSD_HEADS), f32)
    norm_ssd = 1.0 + 0.02 * jax.random.normal(ks[10], (DEPTH, D_SSD), f32)
    norm_in = 1.0 + 0.02 * jax.random.normal(ks[11], (DEPTH, D_MODEL), f32)
    w_out = jax.random.normal(ks[12], (DEPTH, D_INNER, D_MODEL), f32) * D_INNER ** -0.5
    norm_f = 1.0 + 0.02 * jax.random.normal(ks[13], (D_MODEL,), f32)
    return {'x_prompt': x_prompt, 'x_sample': x_sample, 'w_in': w_in, 'conv_w': conv_w,
            'conv_b': conv_b, 'dt_bias_fwd': dt_bias_fwd, 'dt_bias_bwd': dt_bias_bwd,
            'a_log_fwd': a_log_fwd, 'a_log_bwd': a_log_bwd, 'd_skip': d_skip,
            'norm_ssd': norm_ssd, 'norm_in': norm_in, 'w_out': w_out, 'norm_f': norm_f}


def reference(x_prompt, x_sample, w_in, conv_w, conv_b, dt_bias_fwd, dt_bias_bwd,
              a_log_fwd, a_log_bwd, d_skip, norm_ssd, norm_in, w_out, norm_f):
    y_prompt = encoder(x_prompt, w_in, conv_w, conv_b, dt_bias_fwd, dt_bias_bwd, a_log_fwd,
                       a_log_bwd, d_skip, norm_ssd, norm_in, w_out, norm_f)
    y_sample = encoder(x_sample, w_in, conv_w, conv_b, dt_bias_fwd, dt_bias_bwd, a_log_fwd,
                       a_log_bwd, d_skip, norm_ssd, norm_in, w_out, norm_f)
    return (y_prompt, y_sample)
```

```python
import functools

import jax
import jax.numpy as jnp
from jax import lax
from jax.experimental import pallas as pl
from jax.experimental.pallas import tpu as pltpu

D_MODEL = 1024
D_ATTN = 1024
D_SSD = 1024
HEAD_DIM = 64
HEAD_SHIFT = 6
N_HEADS = 16
D_STATE = 128
SSD_GROUPS = 2
CONV_WIDTH = 5
CHUNK = 128
ROPE_THETA = 10000.0
EPS = 1e-5
DILATIONS = (1, 4, 16)
BAND_RADIUS = 64
D_XBC = D_SSD + 2 * SSD_GROUPS * D_STATE
LANES = 128
HALO_ROWS = 16
NEG_BIG = -1e30

F32 = jnp.float32
BF16 = jnp.bfloat16

VMEM_LIMIT = 56 * 1024 * 1024


def _cparams(sem):
    return pltpu.CompilerParams(dimension_semantics=sem, vmem_limit_bytes=VMEM_LIMIT)


def _const_spec(shape):
    nd = len(shape)
    return pl.BlockSpec(shape, lambda *_: (0,) * nd)


def _in_proj_kernel(x_ref, g_ref, cos_ref, sin_ref, wq_ref, wk_ref, wv_ref, wza_ref,
                    wzs_ref, wxbc_ref, wdt_ref,
                    q_ref, k_ref, v_ref, za_ref, zs_ref, xbc_ref, dt_ref):
    x = x_ref[...]
    ms = jnp.mean(x * x, axis=-1, keepdims=True)
    h = (x * lax.rsqrt(ms + EPS) * g_ref[...]).astype(BF16)

    cos = cos_ref[...]
    sin = sin_ref[...]
    lane = lax.broadcasted_iota(jnp.int32, cos.shape, 1)
    first_half = (lane & (HEAD_DIM - 1)) < (HEAD_DIM // 2)

    def rotary(t):
        partner = jnp.where(first_half,
                            pltpu.roll(t, LANES - HEAD_DIM // 2, 1),
                            pltpu.roll(t, HEAD_DIM // 2, 1))
        return t * cos + partner * sin

    for w_ref, o_ref, scale in ((wq_ref, q_ref, HEAD_DIM ** -0.5), (wk_ref, k_ref, 1.0)):
        for j in range(D_ATTN // LANES):
            sl = slice(j * LANES, (j + 1) * LANES)
            t = jnp.dot(h, w_ref[:, sl], preferred_element_type=F32)
            o_ref[:, sl] = (rotary(t) * scale).astype(o_ref.dtype)

    for w_ref, o_ref in ((wv_ref, v_ref), (wza_ref, za_ref), (wzs_ref, zs_ref),
                         (wxbc_ref, xbc_ref), (wdt_ref, dt_ref)):
        o_ref[...] = jnp.dot(h, w_ref[...], preferred_element_type=F32).astype(o_ref.dtype)


def _in_proj(x2, norm_in, cos, sin, ws, seq_len, tm):
    t_total = x2.shape[0]
    blocks_per_seq = seq_len // tm
    wq, wk, wv, wza, wzs, wxbc, wdt = ws
    row = lambda i: (i, 0)
    pos = lambda i: (i % blocks_per_seq, 0)
    out_shapes = (
        [jax.ShapeDtypeStruct((t_total, D_ATTN), BF16)] * 5
        + [jax.ShapeDtypeStruct((t_total, D_XBC), BF16),
           jax.ShapeDtypeStruct((t_total, LANES), F32)])
    return pl.pallas_call(
        _in_proj_kernel,
        grid=(t_total // tm,),
        in_specs=[pl.BlockSpec((tm, D_MODEL), row), _const_spec((1, D_MODEL)),
                  pl.BlockSpec((tm, LANES), pos), pl.BlockSpec((tm, LANES), pos)]
                 + [_const_spec(w.shape) for w in ws],
        out_specs=[pl.BlockSpec((tm, D_ATTN), row)] * 5
                  + [pl.BlockSpec((tm, D_XBC), row), pl.BlockSpec((tm, LANES), row)],
        out_shape=out_shapes,
        compiler_params=_cparams(("arbitrary",)),
        name="in_proj",
    )(x2, norm_in, cos, sin, wq, wk, wv, wza, wzs, wxbc, wdt)


def _silu(x):
    return x / (1.0 + jnp.exp(-x))


def _softplus(x):
    return jnp.maximum(x, 0.0) + jnp.log1p(jnp.exp(-jnp.abs(x)))


def _ssd_kernel(*refs, reverse, final, n_chunks):
    if final:
        (xbc_ref, prev_ref, next_ref, dt_ref, cw_ref, cb_ref, dtb_ref, alog_ref,
         yf_ref, z_ref, dskip_ref, gn_ref, y_ref, ext_scr, state_scr) = refs
    else:
        (xbc_ref, prev_ref, next_ref, dt_ref, cw_ref, cb_ref, dtb_ref, alog_ref,
         y_ref, ext_scr, state_scr) = refs

    step = pl.program_id(1)
    chunk = (n_chunks - 1 - step) if reverse else step
    head_off = N_HEADS if reverse else 0
    last = 0 if reverse else CHUNK - 1

    @pl.when(step == 0)
    def _():
        state_scr[...] = jnp.zeros_like(state_scr)

    prev = prev_ref[...].astype(F32)
    nxt = next_ref[...].astype(F32)
    ext_scr[0:HALO_ROWS, :] = jnp.where(chunk > 0, prev, 0.0)
    ext_scr[HALO_ROWS:HALO_ROWS + CHUNK, :] = xbc_ref[...].astype(F32)
    ext_scr[HALO_ROWS + CHUNK:, :] = jnp.where(chunk < n_chunks - 1, nxt, 0.0)
    pad = (CONV_WIDTH - 1) // 2
    conv = jnp.zeros((CHUNK, D_XBC), F32) + cb_ref[...]
    for j in range(CONV_WIDTH):
        start = HALO_ROWS - pad + j
        conv = conv + ext_scr[start:start + CHUNK, :] * cw_ref[j:j + 1, :]
    act = _silu(conv)
    xs = act[:, :D_SSD]
    b_all = act[:, D_SSD:D_SSD + SSD_GROUPS * D_STATE]
    c_all = act[:, D_SSD + SSD_GROUPS * D_STATE:].astype(BF16)

    dt_col = _softplus(dt_ref[...] + dtb_ref[...])
    a_col = dt_col * (-jnp.exp(alog_ref[...]))
    r_io = lax.broadcasted_iota(jnp.int32, (CHUNK, CHUNK), 0)
    c_io = lax.broadcasted_iota(jnp.int32, (CHUNK, CHUNK), 1)
    mask = (c_io >= r_io) if reverse else (c_io <= r_io)
    tri = jnp.where(mask, 1.0, 0.0).astype(F32)
    acum_col = jnp.dot(tri, a_col, preferred_element_type=F32,
                       precision=lax.Precision.HIGHEST)
    acum_row = acum_col.T

    lane = lax.broadcasted_iota(jnp.int32, (LANES, D_SSD), 1)
    rowi = lax.broadcasted_iota(jnp.int32, (LANES, D_SSD), 0)
    expand = jnp.where(rowi == head_off + lax.shift_right_logical(lane, HEAD_SHIFT),
                       1.0, 0.0).astype(BF16)

    def expand_heads(v):
        return jnp.dot(v.astype(BF16), expand, preferred_element_type=F32)

    ea_x = expand_heads(jnp.exp(acum_col))
    dt_x = expand_heads(dt_col)
    dte_x = expand_heads(jnp.exp(acum_col[last:last + 1, :] - acum_col))
    xdt = xs * dt_x
    xdt_b = xdt.astype(BF16)
    xw_b = (xdt * dte_x).astype(BF16)

    state_in = state_scr[...]
    state_b = state_in.astype(BF16)

    lane2 = lax.broadcasted_iota(jnp.int32, (CHUNK, LANES), 1)
    low_half = lane2 < HEAD_DIM
    zero_b = jnp.zeros((CHUNK, LANES), BF16)

    heads_per_group = N_HEADS // SSD_GROUPS
    gw = heads_per_group * HEAD_DIM
    y_parts = []
    s_parts = []
    for g in range(SSD_GROUPS):
        bg_f = b_all[:, g * D_STATE:(g + 1) * D_STATE]
        bg = bg_f.astype(BF16)
        cg = c_all[:, g * D_STATE:(g + 1) * D_STATE]
        cb = jnp.einsum('ln,sn->ls', cg, bg, preferred_element_type=F32)
        y_off = jnp.dot(cg, state_b[:, g * gw:(g + 1) * gw], preferred_element_type=F32)
        s_parts.append(jnp.dot(bg_f.T.astype(BF16), xw_b[:, g * gw:(g + 1) * gw],
                               preferred_element_type=F32))
        for p in range(heads_per_group // 2):
            h0 = g * heads_per_group + 2 * p
            ms = []
            for hh in (h0, h0 + 1):
                idx = head_off + hh
                seg = acum_col[:, idx:idx + 1] - acum_row[idx:idx + 1, :]
                ms.append(cb * jnp.exp(jnp.where(mask, seg, NEG_BIG)))
            lhs = jnp.concatenate(ms, axis=1).astype(BF16)
            xp = xdt_b[:, h0 * HEAD_DIM:h0 * HEAD_DIM + LANES]
            rhs = jnp.concatenate([jnp.where(low_half, xp, zero_b),
                                   jnp.where(low_half, zero_b, xp)], axis=0)
            y_diag = jnp.dot(lhs, rhs, preferred_element_type=F32)
            sl = slice(p * LANES, (p + 1) * LANES)
            y_parts.append(y_diag + y_off[:, sl] * ea_x[:, g * gw + p * LANES:g * gw + (p + 1) * LANES])

    state_scr[...] = state_in * ea_x[last:last + 1, :] + jnp.concatenate(s_parts, axis=1)
    y = jnp.concatenate(y_parts, axis=1)

    if final:
        y = y + yf_ref[...].astype(F32) + dskip_ref[...] * xs
        y = y * _silu(z_ref[...].astype(F32))
        ms2 = jnp.mean(y * y, axis=-1, keepdims=True)
        y = y * lax.rsqrt(ms2 + EPS) * gn_ref[...]
    y_ref[...] = y.astype(y_ref.dtype)


def _ssd(xbc, dt, conv_w, conv_b, dtb, alog, batch, seq_len, reverse, extra=None):
    t_total = xbc.shape[0]
    n_chunks = seq_len // CHUNK
    halo_per_chunk = CHUNK // HALO_ROWS
    n_halo_blocks = t_total // HALO_ROWS
    final = extra is not None

    def cidx(b, i):
        return b * n_chunks + ((n_chunks - 1 - i) if reverse else i)

    main = lambda b, i: (cidx(b, i), 0)
    prev = lambda b, i: (jnp.maximum(cidx(b, i) * halo_per_chunk - 1, 0), 0)
    nxt = lambda b, i: (jnp.minimum((cidx(b, i) + 1) * halo_per_chunk, n_halo_blocks - 1), 0)
    in_specs = [pl.BlockSpec((CHUNK, D_XBC), main),
                pl.BlockSpec((HALO_ROWS, D_XBC), prev),
                pl.BlockSpec((HALO_ROWS, D_XBC), nxt),
                pl.BlockSpec((CHUNK, LANES), main),
                _const_spec(conv_w.shape), _const_spec(conv_b.shape),
                _const_spec(dtb.shape), _const_spec(alog.shape)]
    args = [xbc, xbc, xbc, dt, conv_w, conv_b, dtb, alog]
    if final:
        y_f, z, dskip_x, gn = extra
        in_specs += [pl.BlockSpec((CHUNK, D_SSD), main), pl.BlockSpec((CHUNK, D_SSD), main),
                     _const_spec(dskip_x.shape), _const_spec(gn.shape)]
        args += [y_f, z, dskip_x, gn]
    return pl.pallas_call(
        functools.partial(_ssd_kernel, reverse=reverse, final=final, n_chunks=n_chunks),
        grid=(batch, n_chunks),
        in_specs=in_specs,
        out_specs=pl.BlockSpec((CHUNK, D_SSD), main),
        out_shape=jax.ShapeDtypeStruct((t_total, D_SSD), BF16),
        scratch_shapes=[pltpu.VMEM((CHUNK + 2 * HALO_ROWS, D_XBC), F32),
                        pltpu.VMEM((D_STATE, D_SSD), F32)],
        compiler_params=_cparams(("arbitrary", "arbitrary")),
        name="ssd_bwd" if reverse else "ssd_fwd",
    )(*args)


Q_SUB = 128
K_WIN = Q_SUB + 2 * BAND_RADIUS


def _band_attn_kernel(q_ref, k_ref, kp_ref, kn_ref, v_ref, vp_ref, vn_ref,
                      o_ref, lse_ref, kbuf, vbuf, *, tl, n_lblocks):
    lb = pl.program_id(2)
    r = BAND_RADIUS
    kbuf[0:r, :] = kp_ref[...]
    kbuf[r:r + tl, :] = k_ref[...]
    kbuf[r + tl:, :] = kn_ref[...]
    vbuf[0:r, :] = vp_ref[...]
    vbuf[r:r + tl, :] = v_ref[...]
    vbuf[r + tl:, :] = vn_ref[...]

    t_io = lax.broadcasted_iota(jnp.int32, (2 * Q_SUB, K_WIN), 0) & (Q_SUB - 1)
    u_io = lax.broadcasted_iota(jnp.int32, (2 * Q_SUB, K_WIN), 1)
    band = (u_io >= t_io) & (u_io <= t_io + 2 * r)
    lane = lax.broadcasted_iota(jnp.int32, (Q_SUB, LANES), 1)
    low_half = lane < HEAD_DIM
    zero_b = jnp.zeros((Q_SUB, LANES), BF16)
    lane_l = lax.broadcasted_iota(jnp.int32, (Q_SUB, LANES), 1)

    n_sub = tl // Q_SUB
    for j in range(n_sub):
        valid = band
        if j == 0:
            valid = valid & (u_io >= jnp.where(lb == 0, r, 0))
        if j == n_sub - 1:
            valid = valid & (u_io < jnp.where(lb == n_lblocks - 1, K_WIN - r, K_WIN))
        lse_tile = jnp.zeros((Q_SUB, LANES), F32)
        rows = slice(j * Q_SUB, (j + 1) * Q_SUB)
        for p in range(N_HEADS // 2):
            cols = slice(p * LANES, (p + 1) * LANES)
            qp = q_ref[rows, cols]
            qs = jnp.concatenate([jnp.where(low_half, qp, zero_b),
                                  jnp.where(low_half, zero_b, qp)], axis=0)
            kw = kbuf[j * Q_SUB:j * Q_SUB + K_WIN, cols]
            vw = vbuf[j * Q_SUB:j * Q_SUB + K_WIN, cols]
            s = jnp.einsum('qd,kd->qk', qs, kw, preferred_element_type=F32)
            s = jnp.where(valid, s, NEG_BIG)
            m = jnp.max(s, axis=-1, keepdims=True)
            e = jnp.exp(s - m)
            l = jnp.sum(e, axis=-1, keepdims=True)
            o2 = jnp.dot(e.astype(BF16), vw, preferred_element_type=F32) / l
            o_ref[rows, cols] = jnp.where(low_half, o2[:Q_SUB], o2[Q_SUB:]).astype(o_ref.dtype)
            lse = m + jnp.log(l)
            lse_tile = jnp.where(lane_l == 2 * p, lse[:Q_SUB], lse_tile)
            lse_tile = jnp.where(lane_l == 2 * p + 1, lse[Q_SUB:], lse_tile)
        lse_ref[rows, :] = lse_tile


def _band_attn(q, k, v, batch, seq_len, dil):
    t_total = q.shape[0]
    sub_len = seq_len // dil
    tl = min(512, sub_len)
    n_lblocks = sub_len // tl
    r = BAND_RADIUS
    halo_per_block = tl // r
    n_rows = t_total // dil
    n_halo = n_rows // r
    qd, kd, vd = (a.reshape(n_rows, dil * D_ATTN) for a in (q, k, v))

    main = lambda b, c, l: (b * n_lblocks + l, c)
    prev = lambda b, c, l: (jnp.maximum((b * n_lblocks + l) * halo_per_block - 1, 0), c)
    nxt = lambda b, c, l: (jnp.minimum((b * n_lblocks + l + 1) * halo_per_block, n_halo - 1), c)
    blk = pl.BlockSpec((tl, D_ATTN), main)
    hp = pl.BlockSpec((r, D_ATTN), prev)
    hn = pl.BlockSpec((r, D_ATTN), nxt)
    o, lse = pl.pallas_call(
        functools.partial(_band_attn_kernel, tl=tl, n_lblocks=n_lblocks),
        grid=(batch, dil, n_lblocks),
        in_specs=[blk, blk, hp, hn, blk, hp, hn],
        out_specs=[blk, pl.BlockSpec((tl, LANES), main)],
        out_shape=[jax.ShapeDtypeStruct((n_rows, dil * D_ATTN), BF16),
                   jax.ShapeDtypeStruct((n_rows, dil * LANES), F32)],
        scratch_shapes=[pltpu.VMEM((tl + 2 * r, D_ATTN), BF16),
                        pltpu.VMEM((tl + 2 * r, D_ATTN), BF16)],
        compiler_params=_cparams(("arbitrary", "arbitrary", "arbitrary")),
        name=f"band_attn_d{dil}",
    )(qd, kd, kd, kd, vd, vd, vd)
    return o.reshape(t_total, D_ATTN), lse.reshape(t_total, LANES)


def _out_proj_kernel(x_ref, ys_ref, o1_ref, o2_ref, o3_ref, l1_ref, l2_ref, l3_ref,
                     za_ref, wo_ref, gf_ref, y_ref):
    l1, l2, l3 = l1_ref[...], l2_ref[...], l3_ref[...]
    m = jnp.maximum(jnp.maximum(l1, l2), l3)
    e1, e2, e3 = jnp.exp(l1 - m), jnp.exp(l2 - m), jnp.exp(l3 - m)
    inv = 1.0 / (e1 + e2 + e3)
    lane = lax.broadcasted_iota(jnp.int32, (LANES, D_ATTN), 1)
    rowi = lax.broadcasted_iota(jnp.int32, (LANES, D_ATTN), 0)
    expand = jnp.where(rowi == lax.shift_right_logical(lane, HEAD_SHIFT), 1.0, 0.0).astype(BF16)
    mix = jnp.zeros(o1_ref.shape, F32)
    for e, o_ref in ((e1, o1_ref), (e2, o2_ref), (e3, o3_ref)):
        w = jnp.dot((e * inv).astype(BF16), expand, preferred_element_type=F32)
        mix = mix + w * o_ref[...].astype(F32)
    z = za_ref[...].astype(F32)
    y_attn = (mix * (z / (1.0 + jnp.exp(-z)))).astype(BF16)
    out = x_ref[...]
    out = out + jnp.dot(ys_ref[...], wo_ref[:D_SSD, :], preferred_element_type=F32)
    out = out + jnp.dot(y_attn, wo_ref[D_SSD:, :], preferred_element_type=F32)
    ms = jnp.mean(out * out, axis=-1, keepdims=True)
    y_ref[...] = out * lax.rsqrt(ms + EPS) * gf_ref[...]


def _out_proj(x2, y_ssd, os_, lses, za, w_out, norm_f, tm):
    t_total = x2.shape[0]
    row = lambda i: (i, 0)
    wide = pl.BlockSpec((tm, D_MODEL), row)
    narrow = pl.BlockSpec((tm, LANES), row)
    return pl.pallas_call(
        _out_proj_kernel,
        grid=(t_total // tm,),
        in_specs=[wide, wide, wide, wide, wide, narrow, narrow, narrow, wide,
                  _const_spec(w_out.shape), _const_spec(norm_f.shape)],
        out_specs=wide,
        out_shape=jax.ShapeDtypeStruct((t_total, D_MODEL), F32),
        compiler_params=_cparams(("arbitrary",)),
        name="out_proj",
    )(x2, y_ssd, *os_, *lses, za, w_out, norm_f)


def _rope_tables(seq_len):
    half = HEAD_DIM // 2
    inv = ROPE_THETA ** (-jnp.arange(half, dtype=F32) * 2.0 / HEAD_DIM)
    ang = jnp.arange(seq_len, dtype=F32)[:, None] * inv[None, :]
    cos, sin = jnp.cos(ang), jnp.sin(ang)
    cos_h = jnp.concatenate([cos, cos], axis=-1)
    sin_h = jnp.concatenate([-sin, sin], axis=-1)
    reps = LANES // HEAD_DIM
    return jnp.tile(cos_h, (1, reps)), jnp.tile(sin_h, (1, reps))


def _pad_lanes(v, offset=0):
    out = jnp.zeros((1, LANES), F32)
    return out.at[0, offset:offset + v.shape[0]].set(v.astype(F32))


def _encoder(x, tables, params):
    (ws, conv_w, conv_b, dtb, alog, dskip_x, norm_ssd, norm_in, w_out, norm_f) = params
    batch, seq_len, _ = x.shape
    t_total = batch * seq_len
    x2 = x.reshape(t_total, D_MODEL)
    cos, sin = tables
    q, k, v, za, zs, xbc, dt = _in_proj(x2, norm_in, cos[:seq_len], sin[:seq_len], ws,
                                        seq_len, tm=256)
    y_f = _ssd(xbc, dt, conv_w, conv_b, dtb, alog, batch, seq_len, reverse=False)
    y_ssd = _ssd(xbc, dt, conv_w, conv_b, dtb, alog, batch, seq_len, reverse=True,
                 extra=(y_f, zs, dskip_x, norm_ssd))
    os_, lses = [], []
    for dil in DILATIONS:
        o, lse = _band_attn(q, k, v, batch, seq_len, dil)
        os_.append(o)
        lses.append(lse)
    y = _out_proj(x2, y_ssd, os_, lses, za, w_out, norm_f, tm=256)
    return y.reshape(batch, seq_len, D_MODEL)


def kernel(x_prompt, x_sample, w_in, conv_w, conv_b, dt_bias_fwd, dt_bias_bwd, a_log_fwd,
           a_log_bwd, d_skip, norm_ssd, norm_in, w_out, norm_f):
    assert w_in.shape[0] == 1, "single-layer encoder"
    w = w_in[0]
    bounds = [0, D_ATTN, 2 * D_ATTN, 3 * D_ATTN, 4 * D_ATTN, 4 * D_ATTN + D_SSD,
              4 * D_ATTN + D_SSD + D_XBC]
    segs = [w[:, bounds[i]:bounds[i + 1]].astype(BF16) for i in range(6)]
    w_dt = jnp.zeros((D_MODEL, LANES), F32).at[:, :2 * N_HEADS].set(w[:, bounds[6]:])
    ws = tuple(segs) + (w_dt.astype(BF16),)
    dtb = _pad_lanes(dt_bias_fwd[0]) + _pad_lanes(dt_bias_bwd[0], N_HEADS)
    alog = _pad_lanes(a_log_fwd[0]) + _pad_lanes(a_log_bwd[0], N_HEADS)
    dskip_x = jnp.repeat(d_skip[0].astype(F32), HEAD_DIM)[None, :]
    params = (ws, conv_w[0].astype(F32), conv_b[0].astype(F32)[None, :], dtb, alog, dskip_x,
              norm_ssd[0].astype(F32)[None, :], norm_in[0].astype(F32)[None, :],
              w_out[0].astype(BF16), norm_f.astype(F32)[None, :])
    tables = _rope_tables(max(x_prompt.shape[1], x_sample.shape[1]))
    return (_encoder(x_prompt, tables, params), _encoder(x_sample, tables, params))
```

```python
import functools

import jax
import jax.numpy as jnp
from jax import lax
from jax.experimental import pallas as pl
from jax.experimental.pallas import tpu as pltpu

D_MODEL = 1024
D_ATTN = 1024
D_SSD = 1024
HEAD_DIM = 64
HEAD_SHIFT = 6
N_HEADS = 16
D_STATE = 128
SSD_GROUPS = 2
CONV_WIDTH = 5
CHUNK = 128
ROPE_THETA = 10000.0
EPS = 1e-5
DILATIONS = (1, 4, 16)
SUB_DIL = 4
BAND_RADIUS = 64
D_XBC = D_SSD + 2 * SSD_GROUPS * D_STATE
LANES = 128
HALO_ROWS = 16
NEG_BIG = -1e30

F32 = jnp.float32
BF16 = jnp.bfloat16

VMEM_LIMIT = 56 * 1024 * 1024
IN_PROJ_ROWS = 512
OUT_PROJ_ROWS = 256


def _cparams(sem):
    return pltpu.CompilerParams(dimension_semantics=sem, vmem_limit_bytes=VMEM_LIMIT)


def _const_spec(shape):
    nd = len(shape)
    return pl.BlockSpec(shape, lambda *_: (0,) * nd)


def _in_proj_kernel(x_ref, g_ref, cos_ref, sin_ref, wq_ref, wk_ref, wv_ref, wza_ref,
                    wzs_ref, wxbc_ref, wdt_ref,
                    q1_ref, q4_ref, q16_ref, k1_ref, k4_ref, k16_ref, v1_ref, v4_ref, v16_ref,
                    za_ref, zs_ref, xbc_ref, dt_ref, blk_scr, sub_scr):
    tm = x_ref.shape[0]

    def emit_dilated(t, j, outs):
        o1_ref, o4_ref, o16_ref = outs
        o1_ref[:, j * LANES:(j + 1) * LANES] = t.astype(BF16)
        blk_scr[...] = t
        for c4 in range(SUB_DIL):
            a = blk_scr[pl.ds(c4, tm // SUB_DIL, stride=SUB_DIL), :]
            col = c4 * D_ATTN + j * LANES
            o4_ref[:, col:col + LANES] = a.astype(BF16)
            sub_scr[c4] = a
        for c4 in range(SUB_DIL):
            for c2 in range(SUB_DIL):
                col = (c4 + SUB_DIL * c2) * D_ATTN + j * LANES
                o16_ref[:, col:col + LANES] = sub_scr[
                    c4, pl.ds(c2, tm // (SUB_DIL * SUB_DIL), stride=SUB_DIL), :].astype(BF16)

    x = x_ref[...]
    ms = jnp.mean(x * x, axis=-1, keepdims=True)
    h = (x * lax.rsqrt(ms + EPS) * g_ref[...]).astype(BF16)

    cos = cos_ref[...]
    sin = sin_ref[...]
    lane = lax.broadcasted_iota(jnp.int32, cos.shape, 1)
    first_half = (lane & (HEAD_DIM - 1)) < (HEAD_DIM // 2)

    def rotary(t):
        partner = jnp.where(first_half,
                            pltpu.roll(t, LANES - HEAD_DIM // 2, 1),
                            pltpu.roll(t, HEAD_DIM // 2, 1))
        return t * cos + partner * sin

    q_outs = (q1_ref, q4_ref, q16_ref)
    k_outs = (k1_ref, k4_ref, k16_ref)
    v_outs = (v1_ref, v4_ref, v16_ref)
    mxu_cols = 2 * LANES
    for jj in range(D_ATTN // mxu_cols):
        sl = slice(jj * mxu_cols, (jj + 1) * mxu_cols)
        tq = jnp.dot(h, wq_ref[:, sl], preferred_element_type=F32)
        tk = jnp.dot(h, wk_ref[:, sl], preferred_element_type=F32)
        tv = jnp.dot(h, wv_ref[:, sl], preferred_element_type=F32)
        for half in range(2):
            hs = slice(half * LANES, (half + 1) * LANES)
            j = 2 * jj + half
            emit_dilated(rotary(tq[:, hs]) * (HEAD_DIM ** -0.5), j, q_outs)
            emit_dilated(rotary(tk[:, hs]), j, k_outs)
            emit_dilated(tv[:, hs], j, v_outs)

    for w_ref, o_ref in ((wza_ref, za_ref), (wzs_ref, zs_ref),
                         (wxbc_ref, xbc_ref), (wdt_ref, dt_ref)):
        o_ref[...] = jnp.dot(h, w_ref[...], preferred_element_type=F32).astype(o_ref.dtype)


def _in_proj(x2, norm_in, cos, sin, ws, seq_len, tm):
    t_total = x2.shape[0]
    blocks_per_seq = seq_len // tm
    row = lambda i: (i, 0)
    pos = lambda i: (i % blocks_per_seq, 0)
    attn_shapes = [jax.ShapeDtypeStruct((t_total // d, d * D_ATTN), BF16) for d in DILATIONS]
    attn_specs = [pl.BlockSpec((tm // d, d * D_ATTN), row) for d in DILATIONS]
    out_shapes = (attn_shapes * 3
                  + [jax.ShapeDtypeStruct((t_total, D_ATTN), BF16)] * 2
                  + [jax.ShapeDtypeStruct((t_total, D_XBC), BF16),
                     jax.ShapeDtypeStruct((t_total, LANES), F32)])
    weight_specs = [pl.BlockSpec(w.shape, lambda i: (0, 0), pipeline_mode=pl.Buffered(1))
                    for w in ws]
    outs = pl.pallas_call(
        _in_proj_kernel,
        grid=(t_total // tm,),
        in_specs=[pl.BlockSpec((tm, D_MODEL), row), _const_spec((1, D_MODEL)),
                  pl.BlockSpec((tm, LANES), pos), pl.BlockSpec((tm, LANES), pos)]
                 + weight_specs,
        out_specs=attn_specs * 3
                  + [pl.BlockSpec((tm, D_ATTN), row)] * 2
                  + [pl.BlockSpec((tm, D_XBC), row), pl.BlockSpec((tm, LANES), row)],
        out_shape=out_shapes,
        scratch_shapes=[pltpu.VMEM((tm, LANES), F32),
                        pltpu.VMEM((SUB_DIL, tm // SUB_DIL, LANES), F32)],
        compiler_params=_cparams(("arbitrary",)),
        name="in_proj",
    )(x2, norm_in, cos, sin, *ws)
    q, k, v = outs[0:3], outs[3:6], outs[6:9]
    return q, k, v, outs[9], outs[10], outs[11], outs[12]


def _silu(x):
    return x / (1.0 + jnp.exp(-x))


def _softplus(x):
    return jnp.maximum(x, 0.0) + jnp.log1p(jnp.exp(-jnp.abs(x)))


def _conv_silu(xbc_ref, prev_ref, next_ref, cw_ref, cb_ref, chunk, n_chunks):
    prev = jnp.where(chunk > 0, prev_ref[...].astype(F32), 0.0)
    nxt = jnp.where(chunk < n_chunks - 1, next_ref[...].astype(F32), 0.0)
    ext = jnp.concatenate([prev, xbc_ref[...].astype(F32), nxt], axis=0)
    n_ext = CHUNK + 2 * HALO_ROWS
    pad = (CONV_WIDTH - 1) // 2
    conv = jnp.zeros((CHUNK, D_XBC), F32) + cb_ref[...]
    for j in range(CONV_WIDTH):
        shift = (pad - j) % n_ext
        rolled = ext if shift == 0 else pltpu.roll(ext, shift, 0)
        conv = conv + rolled[HALO_ROWS:HALO_ROWS + CHUNK, :] * cw_ref[j:j + 1, :]
    return _silu(conv)


def _ssd_kernel(*refs, reverse, n_chunks):
    if reverse:
        (act_in_ref, dt_ref, dtb_ref, alog_ref, yf_ref, z_ref, dskip_ref, gn_ref,
         y_ref, state_scr) = refs
    else:
        (xbc_ref, prev_ref, next_ref, dt_ref, cw_ref, cb_ref, dtb_ref, alog_ref,
         y_ref, act_out_ref, state_scr) = refs

    step = pl.program_id(1)
    chunk = (n_chunks - 1 - step) if reverse else step
    head_off = N_HEADS if reverse else 0
    last = 0 if reverse else CHUNK - 1

    @pl.when(step == 0)
    def _():
        state_scr[...] = jnp.zeros_like(state_scr)

    if reverse:
        act = act_in_ref[...].astype(F32)
    else:
        act = _conv_silu(xbc_ref, prev_ref, next_ref, cw_ref, cb_ref, chunk, n_chunks)
        act_out_ref[...] = act.astype(act_out_ref.dtype)
    xs = act[:, :D_SSD]
    b_all = act[:, D_SSD:D_SSD + SSD_GROUPS * D_STATE]
    c_all = act[:, D_SSD + SSD_GROUPS * D_STATE:].astype(BF16)

    dt_col = _softplus(dt_ref[...] + dtb_ref[...])
    a_col = dt_col * (-jnp.exp(alog_ref[...]))
    r_io = lax.broadcasted_iota(jnp.int32, (CHUNK, CHUNK), 0)
    c_io = lax.broadcasted_iota(jnp.int32, (CHUNK, CHUNK), 1)
    mask = (c_io >= r_io) if reverse else (c_io <= r_io)
    tri = jnp.where(mask, 1.0, 0.0).astype(F32)
    acum_col = jnp.dot(tri, a_col, preferred_element_type=F32,
                       precision=lax.Precision.HIGHEST)
    acum_row = acum_col.T

    lane = lax.broadcasted_iota(jnp.int32, (LANES, D_SSD), 1)
    rowi = lax.broadcasted_iota(jnp.int32, (LANES, D_SSD), 0)
    expand = jnp.where(rowi == head_off + lax.shift_right_logical(lane, HEAD_SHIFT),
                       1.0, 0.0).astype(BF16)

    def expand_heads(v):
        return jnp.dot(v.astype(BF16), expand, preferred_element_type=F32)

    ea_x = expand_heads(jnp.exp(acum_col))
    dt_x = expand_heads(dt_col)
    dte_x = expand_heads(jnp.exp(acum_col[last:last + 1, :] - acum_col))
    xdt = xs * dt_x
    xdt_b = xdt.astype(BF16)
    xw_b = (xdt * dte_x).astype(BF16)

    state_in = state_scr[...]
    state_b = state_in.astype(BF16)

    lane2 = lax.broadcasted_iota(jnp.int32, (CHUNK, LANES), 1)
    low_half = lane2 < HEAD_DIM
    zero_b = jnp.zeros((CHUNK, LANES), BF16)

    heads_per_group = N_HEADS // SSD_GROUPS
    gw = heads_per_group * HEAD_DIM
    y_parts = []
    s_parts = []
    for g in range(SSD_GROUPS):
        bg_f = b_all[:, g * D_STATE:(g + 1) * D_STATE]
        bg = bg_f.astype(BF16)
        cg = c_all[:, g * D_STATE:(g + 1) * D_STATE]
        cb = jnp.einsum('ln,sn->ls', cg, bg, preferred_element_type=F32)
        y_off = jnp.dot(cg, state_b[:, g * gw:(g + 1) * gw], preferred_element_type=F32)
        s_parts.append(jnp.dot(bg_f.T.astype(BF16), xw_b[:, g * gw:(g + 1) * gw],
                               preferred_element_type=F32))
        for p in range(heads_per_group // 2):
            h0 = g * heads_per_group + 2 * p
            ms = []
            for hh in (h0, h0 + 1):
                idx = head_off + hh
                seg = acum_col[:, idx:idx + 1] - acum_row[idx:idx + 1, :]
                ms.append(cb * jnp.exp(jnp.where(mask, seg, NEG_BIG)))
            lhs = jnp.concatenate(ms, axis=1).astype(BF16)
            xp = xdt_b[:, h0 * HEAD_DIM:h0 * HEAD_DIM + LANES]
            rhs = jnp.concatenate([jnp.where(low_half, xp, zero_b),
                                   jnp.where(low_half, zero_b, xp)], axis=0)
            y_diag = jnp.dot(lhs, rhs, preferred_element_type=F32)
            sl = slice(p * LANES, (p + 1) * LANES)
            y_parts.append(y_diag + y_off[:, sl] * ea_x[:, g * gw + p * LANES:g * gw + (p + 1) * LANES])

    state_scr[...] = state_in * ea_x[last:last + 1, :] + jnp.concatenate(s_parts, axis=1)
    y = jnp.concatenate(y_parts, axis=1)

    if reverse:
        y = y + yf_ref[...].astype(F32) + dskip_ref[...] * xs
        y = y * _silu(z_ref[...].astype(F32))
        ms2 = jnp.mean(y * y, axis=-1, keepdims=True)
        y = y * lax.rsqrt(ms2 + EPS) * gn_ref[...]
    y_ref[...] = y.astype(y_ref.dtype)


def _ssd_fwd(xbc, dt, conv_w, conv_b, dtb, alog, batch, seq_len):
    t_total = xbc.shape[0]
    n_chunks = seq_len // CHUNK
    halo_per_chunk = CHUNK // HALO_ROWS
    n_halo_blocks = t_total // HALO_ROWS
    cidx = lambda b, i: b * n_chunks + i
    main = lambda b, i: (cidx(b, i), 0)
    prev = lambda b, i: (jnp.maximum(cidx(b, i) * halo_per_chunk - 1, 0), 0)
    nxt = lambda b, i: (jnp.minimum((cidx(b, i) + 1) * halo_per_chunk, n_halo_blocks - 1), 0)
    return pl.pallas_call(
        functools.partial(_ssd_kernel, reverse=False, n_chunks=n_chunks),
        grid=(batch, n_chunks),
        in_specs=[pl.BlockSpec((CHUNK, D_XBC), main),
                  pl.BlockSpec((HALO_ROWS, D_XBC), prev),
                  pl.BlockSpec((HALO_ROWS, D_XBC), nxt),
                  pl.BlockSpec((CHUNK, LANES), main),
                  _const_spec(conv_w.shape), _const_spec(conv_b.shape),
                  _const_spec(dtb.shape), _const_spec(alog.shape)],
        out_specs=[pl.BlockSpec((CHUNK, D_SSD), main), pl.BlockSpec((CHUNK, D_XBC), main)],
        out_shape=[jax.ShapeDtypeStruct((t_total, D_SSD), BF16),
                   jax.ShapeDtypeStruct((t_total, D_XBC), BF16)],
        scratch_shapes=[pltpu.VMEM((D_STATE, D_SSD), F32)],
        compiler_params=_cparams(("arbitrary", "arbitrary")),
        name="ssd_fwd",
    )(xbc, xbc, xbc, dt, conv_w, conv_b, dtb, alog)


def _ssd_bwd(act, dt, dtb, alog, y_f, z, dskip_x, gn, batch, seq_len):
    t_total = act.shape[0]
    n_chunks = seq_len // CHUNK
    main = lambda b, i: (b * n_chunks + n_chunks - 1 - i, 0)
    wide = pl.BlockSpec((CHUNK, D_SSD), main)
    return pl.pallas_call(
        functools.partial(_ssd_kernel, reverse=True, n_chunks=n_chunks),
        grid=(batch, n_chunks),
        in_specs=[pl.BlockSpec((CHUNK, D_XBC), main), pl.BlockSpec((CHUNK, LANES), main),
                  _const_spec(dtb.shape), _const_spec(alog.shape), wide, wide,
                  _const_spec(dskip_x.shape), _const_spec(gn.shape)],
        out_specs=wide,
        out_shape=jax.ShapeDtypeStruct((t_total, D_SSD), BF16),
        scratch_shapes=[pltpu.VMEM((D_STATE, D_SSD), F32)],
        compiler_params=_cparams(("arbitrary", "arbitrary")),
        name="ssd_bwd",
    )(act, dt, dtb, alog, y_f, z, dskip_x, gn)


Q_SUB = 128
K_WIN = Q_SUB + 2 * BAND_RADIUS


def _band_attn_kernel(q_ref, k_ref, kp_ref, kn_ref, v_ref, vp_ref, vn_ref,
                      o_ref, lse_ref, kbuf, vbuf, *, tl, n_lblocks):
    lb = pl.program_id(2)
    r = BAND_RADIUS
    kbuf[0:r, :] = kp_ref[...]
    kbuf[r:r + tl, :] = k_ref[...]
    kbuf[r + tl:, :] = kn_ref[...]
    vbuf[0:r, :] = vp_ref[...]
    vbuf[r:r + tl, :] = v_ref[...]
    vbuf[r + tl:, :] = vn_ref[...]

    t_io = lax.broadcasted_iota(jnp.int32, (2 * Q_SUB, K_WIN), 0) & (Q_SUB - 1)
    u_io = lax.broadcasted_iota(jnp.int32, (2 * Q_SUB, K_WIN), 1)
    band = (u_io >= t_io) & (u_io <= t_io + 2 * r)
    lane = lax.broadcasted_iota(jnp.int32, (Q_SUB, LANES), 1)
    low_half = lane < HEAD_DIM
    zero_b = jnp.zeros((Q_SUB, LANES), BF16)

    n_sub = tl // Q_SUB
    for j in range(n_sub):
        valid = band
        if j == 0:
            valid = valid & (u_io >= jnp.where(lb == 0, r, 0))
        if j == n_sub - 1:
            valid = valid & (u_io < jnp.where(lb == n_lblocks - 1, K_WIN - r, K_WIN))
        lse_tile = jnp.zeros((Q_SUB, LANES), F32)
        rows = slice(j * Q_SUB, (j + 1) * Q_SUB)
        for p in range(N_HEADS // 2):
            cols = slice(p * LANES, (p + 1) * LANES)
            qp = q_ref[rows, cols]
            qs = jnp.concatenate([jnp.where(low_half, qp, zero_b),
                                  jnp.where(low_half, zero_b, qp)], axis=0)
            kw = kbuf[j * Q_SUB:j * Q_SUB + K_WIN, cols]
            vw = vbuf[j * Q_SUB:j * Q_SUB + K_WIN, cols]
            s = jnp.einsum('qd,kd->qk', qs, kw, preferred_element_type=F32)
            s = jnp.where(valid, s, NEG_BIG)
            m = jnp.max(s, axis=-1, keepdims=True)
            e = jnp.exp(s - m)
            l = jnp.sum(e, axis=-1, keepdims=True)
            o2 = jnp.dot(e.astype(BF16), vw, preferred_element_type=F32) / l
            o_ref[rows, cols] = jnp.where(low_half, o2[:Q_SUB], o2[Q_SUB:]).astype(o_ref.dtype)
            lse = m + jnp.log(l)
            lse_tile = jnp.where(lane == 2 * p, lse[:Q_SUB], lse_tile)
            lse_tile = jnp.where(lane == 2 * p + 1, lse[Q_SUB:], lse_tile)
        lse_ref[rows, :] = lse_tile


def _band_attn(qd, kd, vd, batch, seq_len, dil):
    n_rows = qd.shape[0]
    sub_len = seq_len // dil
    tl = min(512, sub_len)
    n_lblocks = sub_len // tl
    r = BAND_RADIUS
    halo_per_block = tl // r
    n_halo = n_rows // r

    main = lambda b, c, l: (b * n_lblocks + l, c)
    prev = lambda b, c, l: (jnp.maximum((b * n_lblocks + l) * halo_per_block - 1, 0), c)
    nxt = lambda b, c, l: (jnp.minimum((b * n_lblocks + l + 1) * halo_per_block, n_halo - 1), c)
    blk = pl.BlockSpec((tl, D_ATTN), main)
    hp = pl.BlockSpec((r, D_ATTN), prev)
    hn = pl.BlockSpec((r, D_ATTN), nxt)
    return pl.pallas_call(
        functools.partial(_band_attn_kernel, tl=tl, n_lblocks=n_lblocks),
        grid=(batch, dil, n_lblocks),
        in_specs=[blk, blk, hp, hn, blk, hp, hn],
        out_specs=[blk, pl.BlockSpec((tl, LANES), main)],
        out_shape=[jax.ShapeDtypeStruct((n_rows, dil * D_ATTN), BF16),
                   jax.ShapeDtypeStruct((n_rows, dil * LANES), F32)],
        scratch_shapes=[pltpu.VMEM((tl + 2 * r, D_ATTN), BF16),
                        pltpu.VMEM((tl + 2 * r, D_ATTN), BF16)],
        compiler_params=_cparams(("arbitrary", "arbitrary", "arbitrary")),
        name=f"band_attn_d{dil}",
    )(qd, kd, kd, kd, vd, vd, vd)


def _out_proj_kernel(x_ref, ys_ref, o1_ref, o2_ref, o3_ref, l1_ref, l2_ref, l3_ref,
                     za_ref, wo_ref, gf_ref, y_ref, o_scr, omid_scr, l_scr, lmid_scr):
    tm = x_ref.shape[0]

    def token_order(src_ref, scr, mid, dil, width):
        for j in range(width // LANES):
            if dil == SUB_DIL:
                for c in range(dil):
                    col = c * width + j * LANES
                    scr[j, pl.ds(c, tm // dil, stride=dil), :] = (
                        src_ref[:, col:col + LANES].astype(F32))
            else:
                for c4 in range(SUB_DIL):
                    for c2 in range(SUB_DIL):
                        col = (c4 + SUB_DIL * c2) * width + j * LANES
                        mid[j, c4, pl.ds(c2, tm // dil, stride=SUB_DIL), :] = (
                            src_ref[:, col:col + LANES].astype(F32))
                for c4 in range(SUB_DIL):
                    scr[j, pl.ds(c4, tm // SUB_DIL, stride=SUB_DIL), :] = mid[j, c4]
        return jnp.concatenate([scr[j] for j in range(width // LANES)], axis=1)

    l1 = l1_ref[...]
    l2 = token_order(l2_ref, l_scr, lmid_scr, DILATIONS[1], LANES)
    l3 = token_order(l3_ref, l_scr, lmid_scr, DILATIONS[2], LANES)
    m = jnp.maximum(jnp.maximum(l1, l2), l3)
    e1, e2, e3 = jnp.exp(l1 - m), jnp.exp(l2 - m), jnp.exp(l3 - m)
    inv = 1.0 / (e1 + e2 + e3)
    lane = lax.broadcasted_iota(jnp.int32, (LANES, D_ATTN), 1)
    rowi = lax.broadcasted_iota(jnp.int32, (LANES, D_ATTN), 0)
    expand = jnp.where(rowi == lax.shift_right_logical(lane, HEAD_SHIFT), 1.0, 0.0).astype(BF16)

    def weight(e):
        return jnp.dot((e * inv).astype(BF16), expand, preferred_element_type=F32)

    mix = weight(e1) * o1_ref[...].astype(F32)
    mix = mix + weight(e2) * token_order(o2_ref, o_scr, omid_scr, DILATIONS[1], D_ATTN)
    mix = mix + weight(e3) * token_order(o3_ref, o_scr, omid_scr, DILATIONS[2], D_ATTN)
    z = za_ref[...].astype(F32)
    y_attn = (mix * (z / (1.0 + jnp.exp(-z)))).astype(BF16)
    out = x_ref[...]
    out = out + jnp.dot(ys_ref[...], wo_ref[:D_SSD, :], preferred_element_type=F32)
    out = out + jnp.dot(y_attn, wo_ref[D_SSD:, :], preferred_element_type=F32)
    ms = jnp.mean(out * out, axis=-1, keepdims=True)
    y_ref[...] = out * lax.rsqrt(ms + EPS) * gf_ref[...]


def _out_proj(x2, y_ssd, os_, lses, za, w_out, norm_f, tm):
    t_total = x2.shape[0]
    row = lambda i: (i, 0)
    wide = pl.BlockSpec((tm, D_MODEL), row)
    o_specs = [pl.BlockSpec((tm // d, d * D_ATTN), row) for d in DILATIONS]
    l_specs = [pl.BlockSpec((tm // d, d * LANES), row) for d in DILATIONS]
    return pl.pallas_call(
        _out_proj_kernel,
        grid=(t_total // tm,),
        in_specs=[wide, wide] + o_specs + l_specs
                 + [wide, _const_spec(w_out.shape), _const_spec(norm_f.shape)],
        out_specs=wide,
        out_shape=jax.ShapeDtypeStruct((t_total, D_MODEL), F32),
        scratch_shapes=[pltpu.VMEM((D_ATTN // LANES, tm, LANES), F32),
                        pltpu.VMEM((D_ATTN // LANES, SUB_DIL, tm // SUB_DIL, LANES), F32),
                        pltpu.VMEM((1, tm, LANES), F32),
                        pltpu.VMEM((1, SUB_DIL, tm // SUB_DIL, LANES), F32)],
        compiler_params=_cparams(("arbitrary",)),
        name="out_proj",
    )(x2, y_ssd, *os_, *lses, za, w_out, norm_f)


def _rope_tables(seq_len):
    half = HEAD_DIM // 2
    inv = ROPE_THETA ** (-jnp.arange(half, dtype=F32) * 2.0 / HEAD_DIM)
    ang = jnp.arange(seq_len, dtype=F32)[:, None] * inv[None, :]
    cos, sin = jnp.cos(ang), jnp.sin(ang)
    cos_h = jnp.concatenate([cos, cos], axis=-1)
    sin_h = jnp.concatenate([-sin, sin], axis=-1)
    reps = LANES // HEAD_DIM
    return jnp.tile(cos_h, (1, reps)), jnp.tile(sin_h, (1, reps))


def _pad_lanes(v, offset=0):
    out = jnp.zeros((1, LANES), F32)
    return out.at[0, offset:offset + v.shape[0]].set(v.astype(F32))


def _encoder(x, tables, params):
    (ws, conv_w, conv_b, dtb, alog, dskip_x, norm_ssd, norm_in, w_out, norm_f) = params
    batch, seq_len, _ = x.shape
    t_total = batch * seq_len
    x2 = x.reshape(t_total, D_MODEL)
    cos, sin = tables
    q, k, v, za, zs, xbc, dt = _in_proj(x2, norm_in, cos[:seq_len], sin[:seq_len], ws,
                                        seq_len, tm=IN_PROJ_ROWS)
    y_f, act = _ssd_fwd(xbc, dt, conv_w, conv_b, dtb, alog, batch, seq_len)
    y_ssd = _ssd_bwd(act, dt, dtb, alog, y_f, zs, dskip_x, norm_ssd, batch, seq_len)
    os_, lses = [], []
    for i, dil in enumerate(DILATIONS):
        o, lse = _band_attn(q[i], k[i], v[i], batch, seq_len, dil)
        os_.append(o)
        lses.append(lse)
    y = _out_proj(x2, y_ssd, os_, lses, za, w_out, norm_f, tm=OUT_PROJ_ROWS)
    return y.reshape(batch, seq_len, D_MODEL)


def kernel(x_prompt, x_sample, w_in, conv_w, conv_b, dt_bias_fwd, dt_bias_bwd, a_log_fwd,
           a_log_bwd, d_skip, norm_ssd, norm_in, w_out, norm_f):
    assert w_in.shape[0] == 1, "single-layer encoder"
    w = w_in[0]
    bounds = [0, D_ATTN, 2 * D_ATTN, 3 * D_ATTN, 4 * D_ATTN, 4 * D_ATTN + D_SSD,
              4 * D_ATTN + D_SSD + D_XBC]
    segs = [w[:, bounds[i]:bounds[i + 1]].astype(BF16) for i in range(6)]
    w_dt = jnp.zeros((D_MODEL, LANES), F32).at[:, :2 * N_HEADS].set(w[:, bounds[6]:])
    ws = tuple(segs) + (w_dt.astype(BF16),)
    dtb = _pad_lanes(dt_bias_fwd[0]) + _pad_lanes(dt_bias_bwd[0], N_HEADS)
    alog = _pad_lanes(a_log_fwd[0]) + _pad_lanes(a_log_bwd[0], N_HEADS)
    dskip_x = jnp.repeat(d_skip[0].astype(F32), HEAD_DIM)[None, :]
    params = (ws, conv_w[0].astype(F32), conv_b[0].astype(F32)[None, :], dtb, alog, dskip_x,
              norm_ssd[0].astype(F32)[None, :], norm_in[0].astype(F32)[None, :],
              w_out[0].astype(BF16), norm_f.astype(F32)[None, :])
    tables = _rope_tables(max(x_prompt.shape[1], x_sample.shape[1]))
    return (_encoder(x_prompt, tables, params), _encoder(x_sample, tables, params))
```

```python
import functools

import jax
import jax.numpy as jnp
from jax import lax
from jax.experimental import pallas as pl
from jax.experimental.pallas import tpu as pltpu

D_MODEL = 1024
D_ATTN = 1024
D_SSD = 1024
HEAD_DIM = 64
HEAD_SHIFT = 6
N_HEADS = 16
D_STATE = 128
SSD_GROUPS = 2
CONV_WIDTH = 5
CHUNK = 128
ROPE_THETA = 10000.0
EPS = 1e-5
DILATIONS = (1, 4, 16)
SUB_DIL = 4
BAND_RADIUS = 64
D_XBC = D_SSD + 2 * SSD_GROUPS * D_STATE
LANES = 128
HALO_ROWS = 16
NEG_BIG = -1e30
LOG2E = 1.4426950408889634
Q_SCALE = HEAD_DIM ** -0.5 * LOG2E

F32 = jnp.float32
BF16 = jnp.bfloat16

VMEM_LIMIT = 56 * 1024 * 1024
IN_PROJ_ROWS = 512
OUT_PROJ_ROWS = 512
SSD_CHUNKS_PER_STEP = 4


def _cparams(sem):
    return pltpu.CompilerParams(dimension_semantics=sem, vmem_limit_bytes=VMEM_LIMIT)


def _const_spec(shape):
    nd = len(shape)
    return pl.BlockSpec(shape, lambda *_: (0,) * nd)


def _in_proj_kernel(x_ref, g_ref, cos_ref, sin_ref, wq_ref, wk_ref, wv_ref, wza_ref,
                    wzs_ref, wxbc_ref, wdt_ref,
                    q1_ref, q4_ref, q16_ref, k1_ref, k4_ref, k16_ref, v1_ref, v4_ref, v16_ref,
                    za_ref, zs_ref, xbc_ref, dt_ref, blk_scr, sub_scr):
    tm = x_ref.shape[0]

    def emit_dilated(t, j, outs):
        o1_ref, o4_ref, o16_ref = outs
        o1_ref[:, j * LANES:(j + 1) * LANES] = t.astype(BF16)
        blk_scr[...] = t
        for c4 in range(SUB_DIL):
            a = blk_scr[pl.ds(c4, tm // SUB_DIL, stride=SUB_DIL), :]
            col = c4 * D_ATTN + j * LANES
            o4_ref[:, col:col + LANES] = a.astype(BF16)
            sub_scr[c4] = a
        for c4 in range(SUB_DIL):
            for c2 in range(SUB_DIL):
                col = (c4 + SUB_DIL * c2) * D_ATTN + j * LANES
                o16_ref[:, col:col + LANES] = sub_scr[
                    c4, pl.ds(c2, tm // (SUB_DIL * SUB_DIL), stride=SUB_DIL), :].astype(BF16)

    x = x_ref[...]
    ms = jnp.mean(x * x, axis=-1, keepdims=True)
    h = (x * lax.rsqrt(ms + EPS) * g_ref[...]).astype(BF16)

    cos = cos_ref[...]
    sin = sin_ref[...]
    lane = lax.broadcasted_iota(jnp.int32, cos.shape, 1)
    first_half = (lane & (HEAD_DIM - 1)) < (HEAD_DIM // 2)

    def rotary(t):
        partner = jnp.where(first_half,
                            pltpu.roll(t, LANES - HEAD_DIM // 2, 1),
                            pltpu.roll(t, HEAD_DIM // 2, 1))
        return t * cos + partner * sin

    q_outs = (q1_ref, q4_ref, q16_ref)
    k_outs = (k1_ref, k4_ref, k16_ref)
    v_outs = (v1_ref, v4_ref, v16_ref)
    mxu_cols = 2 * LANES
    for jj in range(D_ATTN // mxu_cols):
        sl = slice(jj * mxu_cols, (jj + 1) * mxu_cols)
        tq = jnp.dot(h, wq_ref[:, sl], preferred_element_type=F32)
        tk = jnp.dot(h, wk_ref[:, sl], preferred_element_type=F32)
        tv = jnp.dot(h, wv_ref[:, sl], preferred_element_type=F32)
        for half in range(2):
            hs = slice(half * LANES, (half + 1) * LANES)
            j = 2 * jj + half
            emit_dilated(rotary(tq[:, hs]) * Q_SCALE, j, q_outs)
            emit_dilated(rotary(tk[:, hs]), j, k_outs)
            emit_dilated(tv[:, hs], j, v_outs)

    for w_ref, o_ref in ((wza_ref, za_ref), (wzs_ref, zs_ref),
                         (wxbc_ref, xbc_ref), (wdt_ref, dt_ref)):
        o_ref[...] = jnp.dot(h, w_ref[...], preferred_element_type=F32).astype(o_ref.dtype)


def _in_proj(x2, norm_in, cos, sin, ws, seq_len, tm):
    t_total = x2.shape[0]
    blocks_per_seq = seq_len // tm
    row = lambda i: (i, 0)
    pos = lambda i: (i % blocks_per_seq, 0)
    attn_shapes = [jax.ShapeDtypeStruct((t_total // d, d * D_ATTN), BF16) for d in DILATIONS]
    attn_specs = [pl.BlockSpec((tm // d, d * D_ATTN), row) for d in DILATIONS]
    out_shapes = (attn_shapes * 3
                  + [jax.ShapeDtypeStruct((t_total, D_ATTN), BF16)] * 2
                  + [jax.ShapeDtypeStruct((t_total, D_XBC), BF16),
                     jax.ShapeDtypeStruct((t_total, LANES), F32)])
    weight_specs = [pl.BlockSpec(w.shape, lambda i: (0, 0), pipeline_mode=pl.Buffered(1))
                    for w in ws]
    outs = pl.pallas_call(
        _in_proj_kernel,
        grid=(t_total // tm,),
        in_specs=[pl.BlockSpec((tm, D_MODEL), row), _const_spec((1, D_MODEL)),
                  pl.BlockSpec((tm, LANES), pos), pl.BlockSpec((tm, LANES), pos)]
                 + weight_specs,
        out_specs=attn_specs * 3
                  + [pl.BlockSpec((tm, D_ATTN), row)] * 2
                  + [pl.BlockSpec((tm, D_XBC), row), pl.BlockSpec((tm, LANES), row)],
        out_shape=out_shapes,
        scratch_shapes=[pltpu.VMEM((tm, LANES), F32),
                        pltpu.VMEM((SUB_DIL, tm // SUB_DIL, LANES), F32)],
        compiler_params=_cparams(("arbitrary",)),
        name="in_proj",
    )(x2, norm_in, cos, sin, *ws)
    q, k, v = outs[0:3], outs[3:6], outs[6:9]
    return q, k, v, outs[9], outs[10], outs[11], outs[12]


def _silu(x):
    return x / (1.0 + jnp.exp(-x))


def _softplus(x):
    return jnp.maximum(x, 0.0) + jnp.log1p(jnp.exp(-jnp.abs(x)))


def _conv_silu(ext, cw_ref, cb_ref):
    n_ext = ext.shape[0]
    rows = n_ext - 2 * HALO_ROWS
    pad = (CONV_WIDTH - 1) // 2
    conv = jnp.zeros((rows, D_XBC), F32) + cb_ref[...]
    for j in range(CONV_WIDTH):
        shift = (pad - j) % n_ext
        rolled = ext if shift == 0 else pltpu.roll(ext, shift, 0)
        conv = conv + rolled[HALO_ROWS:HALO_ROWS + rows, :] * cw_ref[j:j + 1, :]
    return _silu(conv)


def _ssd_kernel(*refs, reverse, n_steps, cps):
    if reverse:
        (act_in_ref, dt_ref, dtb_ref, alog_ref, yf_ref, z_ref, dskip_ref, gn_ref,
         y_ref, state_scr) = refs
    else:
        (xbc_ref, prev_ref, next_ref, dt_ref, cw_ref, cb_ref, dtb_ref, alog_ref,
         y_ref, act_out_ref, state_scr) = refs

    step = pl.program_id(1)

    @pl.when(step == 0)
    def _():
        state_scr[...] = jnp.zeros_like(state_scr)

    if not reverse:
        prev = jnp.where(step > 0, prev_ref[...].astype(F32), 0.0)
        nxt = jnp.where(step < n_steps - 1, next_ref[...].astype(F32), 0.0)

    for ci in range(cps):
        c = (cps - 1 - ci) if reverse else ci
        rows = slice(c * CHUNK, (c + 1) * CHUNK)
        if reverse:
            act = act_in_ref[rows, :].astype(F32)
            extra = (yf_ref[rows, :], z_ref[rows, :], dskip_ref, gn_ref)
        else:
            lo = prev if c == 0 else xbc_ref[c * CHUNK - HALO_ROWS:c * CHUNK, :].astype(F32)
            hi = (nxt if c == cps - 1
                  else xbc_ref[(c + 1) * CHUNK:(c + 1) * CHUNK + HALO_ROWS, :].astype(F32))
            ext = jnp.concatenate([lo, xbc_ref[rows, :].astype(F32), hi], axis=0)
            act = _conv_silu(ext, cw_ref, cb_ref)
            act_out_ref[rows, :] = act.astype(act_out_ref.dtype)
            extra = None
        y = _ssd_chunk(act, dt_ref[rows, :], dtb_ref, alog_ref, state_scr, reverse, extra)
        y_ref[rows, :] = y.astype(y_ref.dtype)


def _ssd_chunk(act, dt_raw, dtb_ref, alog_ref, state_scr, reverse, extra):
    head_off = N_HEADS if reverse else 0
    last = 0 if reverse else CHUNK - 1
    xs = act[:, :D_SSD]
    b_all = act[:, D_SSD:D_SSD + SSD_GROUPS * D_STATE]
    c_all = act[:, D_SSD + SSD_GROUPS * D_STATE:].astype(BF16)

    dt_col = _softplus(dt_raw + dtb_ref[...])
    a_col = dt_col * (-jnp.exp(alog_ref[...]))
    r_io = lax.broadcasted_iota(jnp.int32, (CHUNK, CHUNK), 0)
    c_io = lax.broadcasted_iota(jnp.int32, (CHUNK, CHUNK), 1)
    mask = (c_io >= r_io) if reverse else (c_io <= r_io)
    tri = jnp.where(mask, 1.0, 0.0).astype(F32)
    acum_col = jnp.dot(tri, a_col, preferred_element_type=F32,
                       precision=lax.Precision.HIGHEST)
    acum_row = acum_col.T

    lane = lax.broadcasted_iota(jnp.int32, (LANES, D_SSD), 1)
    rowi = lax.broadcasted_iota(jnp.int32, (LANES, D_SSD), 0)
    expand = jnp.where(rowi == head_off + lax.shift_right_logical(lane, HEAD_SHIFT),
                       1.0, 0.0).astype(BF16)

    def expand_heads(v):
        return jnp.dot(v.astype(BF16), expand, preferred_element_type=F32)

    ea_x = expand_heads(jnp.exp(acum_col))
    dt_x = expand_heads(dt_col)
    dte_x = expand_heads(jnp.exp(acum_col[last:last + 1, :] - acum_col))
    xdt = xs * dt_x
    xdt_b = xdt.astype(BF16)
    xw_b = (xdt * dte_x).astype(BF16)

    state_in = state_scr[...]
    state_b = state_in.astype(BF16)

    lane2 = lax.broadcasted_iota(jnp.int32, (CHUNK, LANES), 1)
    low_half = lane2 < HEAD_DIM
    zero_b = jnp.zeros((CHUNK, LANES), BF16)

    heads_per_group = N_HEADS // SSD_GROUPS
    gw = heads_per_group * HEAD_DIM
    y_parts = []
    s_parts = []
    for g in range(SSD_GROUPS):
        bg_f = b_all[:, g * D_STATE:(g + 1) * D_STATE]
        bg = bg_f.astype(BF16)
        cg = c_all[:, g * D_STATE:(g + 1) * D_STATE]
        cb = jnp.einsum('ln,sn->ls', cg, bg, preferred_element_type=F32)
        y_off = jnp.dot(cg, state_b[:, g * gw:(g + 1) * gw], preferred_element_type=F32)
        s_parts.append(jnp.dot(bg_f.T.astype(BF16), xw_b[:, g * gw:(g + 1) * gw],
                               preferred_element_type=F32))
        for p in range(heads_per_group // 2):
            h0 = g * heads_per_group + 2 * p
            ms = []
            for hh in (h0, h0 + 1):
                idx = head_off + hh
                seg = acum_col[:, idx:idx + 1] - acum_row[idx:idx + 1, :]
                ms.append(cb * jnp.exp(jnp.where(mask, seg, NEG_BIG)))
            lhs = jnp.concatenate(ms, axis=1).astype(BF16)
            xp = xdt_b[:, h0 * HEAD_DIM:h0 * HEAD_DIM + LANES]
            rhs = jnp.concatenate([jnp.where(low_half, xp, zero_b),
                                   jnp.where(low_half, zero_b, xp)], axis=0)
            y_diag = jnp.dot(lhs, rhs, preferred_element_type=F32)
            sl = slice(p * LANES, (p + 1) * LANES)
            y_parts.append(y_diag + y_off[:, sl] * ea_x[:, g * gw + p * LANES:g * gw + (p + 1) * LANES])

    state_scr[...] = state_in * ea_x[last:last + 1, :] + jnp.concatenate(s_parts, axis=1)
    y = jnp.concatenate(y_parts, axis=1)

    if reverse:
        yf, z, dskip_ref, gn_ref = extra
        y = y + yf.astype(F32) + dskip_ref[...] * xs
        y = y * _silu(z.astype(F32))
        ms2 = jnp.mean(y * y, axis=-1, keepdims=True)
        y = y * lax.rsqrt(ms2 + EPS) * gn_ref[...]
    return y


def _ssd_fwd(xbc, dt, conv_w, conv_b, dtb, alog, batch, seq_len):
    t_total = xbc.shape[0]
    cps = SSD_CHUNKS_PER_STEP
    rows = cps * CHUNK
    n_steps = seq_len // rows
    halo_per_step = rows // HALO_ROWS
    n_halo_blocks = t_total // HALO_ROWS
    sidx = lambda b, i: b * n_steps + i
    main = lambda b, i: (sidx(b, i), 0)
    prev = lambda b, i: (jnp.maximum(sidx(b, i) * halo_per_step - 1, 0), 0)
    nxt = lambda b, i: (jnp.minimum((sidx(b, i) + 1) * halo_per_step, n_halo_blocks - 1), 0)
    return pl.pallas_call(
        functools.partial(_ssd_kernel, reverse=False, n_steps=n_steps, cps=cps),
        grid=(batch, n_steps),
        in_specs=[pl.BlockSpec((rows, D_XBC), main),
                  pl.BlockSpec((HALO_ROWS, D_XBC), prev),
                  pl.BlockSpec((HALO_ROWS, D_XBC), nxt),
                  pl.BlockSpec((rows, LANES), main),
                  _const_spec(conv_w.shape), _const_spec(conv_b.shape),
                  _const_spec(dtb.shape), _const_spec(alog.shape)],
        out_specs=[pl.BlockSpec((rows, D_SSD), main), pl.BlockSpec((rows, D_XBC), main)],
        out_shape=[jax.ShapeDtypeStruct((t_total, D_SSD), BF16),
                   jax.ShapeDtypeStruct((t_total, D_XBC), BF16)],
        scratch_shapes=[pltpu.VMEM((D_STATE, D_SSD), F32)],
        compiler_params=_cparams(("arbitrary", "arbitrary")),
        name="ssd_fwd",
    )(xbc, xbc, xbc, dt, conv_w, conv_b, dtb, alog)


def _ssd_bwd(act, dt, dtb, alog, y_f, z, dskip_x, gn, batch, seq_len):
    t_total = act.shape[0]
    cps = SSD_CHUNKS_PER_STEP
    rows = cps * CHUNK
    n_steps = seq_len // rows
    main = lambda b, i: (b * n_steps + n_steps - 1 - i, 0)
    wide = pl.BlockSpec((rows, D_SSD), main)
    return pl.pallas_call(
        functools.partial(_ssd_kernel, reverse=True, n_steps=n_steps, cps=cps),
        grid=(batch, n_steps),
        in_specs=[pl.BlockSpec((rows, D_XBC), main), pl.BlockSpec((rows, LANES), main),
                  _const_spec(dtb.shape), _const_spec(alog.shape), wide, wide,
                  _const_spec(dskip_x.shape), _const_spec(gn.shape)],
        out_specs=wide,
        out_shape=jax.ShapeDtypeStruct((t_total, D_SSD), BF16),
        scratch_shapes=[pltpu.VMEM((D_STATE, D_SSD), F32)],
        compiler_params=_cparams(("arbitrary", "arbitrary")),
        name="ssd_bwd",
    )(act, dt, dtb, alog, y_f, z, dskip_x, gn)


Q_SUB = 128
K_WIN = Q_SUB + 2 * BAND_RADIUS


def _band_attn_kernel(q_ref, k_ref, kp_ref, kn_ref, v_ref, vp_ref, vn_ref,
                      o_ref, stat_ref, kbuf, vbuf, bias_scr, *, tl, n_lblocks):
    lb = pl.program_id(2)
    r = BAND_RADIUS
    kbuf[0:r, :] = kp_ref[...]
    kbuf[r:r + tl, :] = k_ref[...]
    kbuf[r + tl:, :] = kn_ref[...]
    vbuf[0:r, :] = vp_ref[...]
    vbuf[r:r + tl, :] = v_ref[...]
    vbuf[r + tl:, :] = vn_ref[...]

    @pl.when((pl.program_id(0) == 0) & (pl.program_id(1) == 0) & (lb == 0))
    def _():
        t_io = lax.broadcasted_iota(jnp.int32, (2 * Q_SUB, K_WIN), 0) & (Q_SUB - 1)
        u_io = lax.broadcasted_iota(jnp.int32, (2 * Q_SUB, K_WIN), 1)
        band = (u_io >= t_io) & (u_io <= t_io + 2 * r)
        bias_scr[0] = jnp.where(band, 0.0, NEG_BIG)
        bias_scr[1] = jnp.where(band & (u_io >= r), 0.0, NEG_BIG)
        bias_scr[2] = jnp.where(band & (u_io < K_WIN - r), 0.0, NEG_BIG)

    lane = lax.broadcasted_iota(jnp.int32, (Q_SUB, LANES), 1)
    low_half = lane < HEAD_DIM
    zero_b = jnp.zeros((Q_SUB, LANES), BF16)

    n_sub = tl // Q_SUB
    assert n_sub >= 2, "first and last score tiles of a block must differ"
    for j in range(n_sub):
        if j == 0:
            slot = jnp.where(lb == 0, 1, 0)
        elif j == n_sub - 1:
            slot = jnp.where(lb == n_lblocks - 1, 2, 0)
        else:
            slot = 0
        stat_tile = jnp.zeros((Q_SUB, LANES), F32)
        rows = slice(j * Q_SUB, (j + 1) * Q_SUB)
        for p in range(N_HEADS // 2):
            cols = slice(p * LANES, (p + 1) * LANES)
            qp = q_ref[rows, cols]
            qs = jnp.concatenate([jnp.where(low_half, qp, zero_b),
                                  jnp.where(low_half, zero_b, qp)], axis=0)
            kw = kbuf[j * Q_SUB:j * Q_SUB + K_WIN, cols]
            vw = vbuf[j * Q_SUB:j * Q_SUB + K_WIN, cols]
            s = jnp.einsum('qd,kd->qk', qs, kw, preferred_element_type=F32) + bias_scr[slot]
            m = jnp.max(s, axis=-1, keepdims=True)
            e = jnp.exp2(s - m)
            l = jnp.sum(e, axis=-1, keepdims=True)
            o2 = jnp.dot(e.astype(BF16), vw, preferred_element_type=F32)
            o_ref[rows, cols] = jnp.where(low_half, o2[:Q_SUB], o2[Q_SUB:]).astype(o_ref.dtype)
            for hh, sl in ((2 * p, slice(0, Q_SUB)), (2 * p + 1, slice(Q_SUB, 2 * Q_SUB))):
                stat_tile = jnp.where(lane == hh, m[sl], stat_tile)
                stat_tile = jnp.where(lane == N_HEADS + hh, l[sl], stat_tile)
        stat_ref[rows, :] = stat_tile


def _band_attn(qd, kd, vd, batch, seq_len, dil):
    n_rows = qd.shape[0]
    sub_len = seq_len // dil
    tl = min(512, sub_len)
    n_lblocks = sub_len // tl
    r = BAND_RADIUS
    halo_per_block = tl // r
    n_halo = n_rows // r

    main = lambda b, c, l: (b * n_lblocks + l, c)
    prev = lambda b, c, l: (jnp.maximum((b * n_lblocks + l) * halo_per_block - 1, 0), c)
    nxt = lambda b, c, l: (jnp.minimum((b * n_lblocks + l + 1) * halo_per_block, n_halo - 1), c)
    blk = pl.BlockSpec((tl, D_ATTN), main)
    hp = pl.BlockSpec((r, D_ATTN), prev)
    hn = pl.BlockSpec((r, D_ATTN), nxt)
    return pl.pallas_call(
        functools.partial(_band_attn_kernel, tl=tl, n_lblocks=n_lblocks),
        grid=(batch, dil, n_lblocks),
        in_specs=[blk, blk, hp, hn, blk, hp, hn],
        out_specs=[blk, pl.BlockSpec((tl, LANES), main)],
        out_shape=[jax.ShapeDtypeStruct((n_rows, dil * D_ATTN), BF16),
                   jax.ShapeDtypeStruct((n_rows, dil * LANES), F32)],
        scratch_shapes=[pltpu.VMEM((tl + 2 * r, D_ATTN), BF16),
                        pltpu.VMEM((tl + 2 * r, D_ATTN), BF16),
                        pltpu.VMEM((3, 2 * Q_SUB, K_WIN), F32)],
        compiler_params=_cparams(("arbitrary", "arbitrary", "arbitrary")),
        name=f"band_attn_d{dil}",
    )(qd, kd, kd, kd, vd, vd, vd)


def _out_proj_kernel(x_ref, ys_ref, o1_ref, o2_ref, o3_ref, l1_ref, l2_ref, l3_ref,
                     za_ref, wo_ref, gf_ref, y_ref, o_scr, omid_scr, l_scr, lmid_scr):
    tm = x_ref.shape[0]

    def token_order(src_ref, scr, mid, dil, width):
        for j in range(width // LANES):
            if dil == SUB_DIL:
                for c in range(dil):
                    col = c * width + j * LANES
                    scr[j, pl.ds(c, tm // dil, stride=dil), :] = (
                        src_ref[:, col:col + LANES].astype(F32))
            else:
                for c4 in range(SUB_DIL):
                    for c2 in range(SUB_DIL):
                        col = (c4 + SUB_DIL * c2) * width + j * LANES
                        mid[j, c4, pl.ds(c2, tm // dil, stride=SUB_DIL), :] = (
                            src_ref[:, col:col + LANES].astype(F32))
                for c4 in range(SUB_DIL):
                    scr[j, pl.ds(c4, tm // SUB_DIL, stride=SUB_DIL), :] = mid[j, c4]
        return jnp.concatenate([scr[j] for j in range(width // LANES)], axis=1)

    s1 = l1_ref[...]
    s2 = token_order(l2_ref, l_scr, lmid_scr, DILATIONS[1], LANES)
    s3 = token_order(l3_ref, l_scr, lmid_scr, DILATIONS[2], LANES)
    m = jnp.maximum(jnp.maximum(s1, s2), s3)
    head_lane = lax.broadcasted_iota(jnp.int32, s1.shape, 1) < N_HEADS
    a = [jnp.exp2(s - m) for s in (s1, s2, s3)]
    den = sum(ap * pltpu.roll(s, LANES - N_HEADS, 1) for ap, s in zip(a, (s1, s2, s3)))
    inv = jnp.where(head_lane, 1.0 / jnp.where(head_lane, den, 1.0), 0.0)
    lane = lax.broadcasted_iota(jnp.int32, (LANES, D_ATTN), 1)
    rowi = lax.broadcasted_iota(jnp.int32, (LANES, D_ATTN), 0)
    expand = jnp.where(rowi == lax.shift_right_logical(lane, HEAD_SHIFT), 1.0, 0.0).astype(BF16)

    def weight(ap):
        return jnp.dot((ap * inv).astype(BF16), expand, preferred_element_type=F32)

    mix = weight(a[0]) * o1_ref[...].astype(F32)
    mix = mix + weight(a[1]) * token_order(o2_ref, o_scr, omid_scr, DILATIONS[1], D_ATTN)
    mix = mix + weight(a[2]) * token_order(o3_ref, o_scr, omid_scr, DILATIONS[2], D_ATTN)
    z = za_ref[...].astype(F32)
    y_attn = (mix * (z / (1.0 + jnp.exp(-z)))).astype(BF16)
    out = x_ref[...]
    out = out + jnp.dot(ys_ref[...], wo_ref[:D_SSD, :], preferred_element_type=F32)
    out = out + jnp.dot(y_attn, wo_ref[D_SSD:, :], preferred_element_type=F32)
    ms = jnp.mean(out * out, axis=-1, keepdims=True)
    y_ref[...] = out * lax.rsqrt(ms + EPS) * gf_ref[...]


def _out_proj(x2, y_ssd, os_, lses, za, w_out, norm_f, tm):
    t_total = x2.shape[0]
    row = lambda i: (i, 0)
    wide = pl.BlockSpec((tm, D_MODEL), row)
    o_specs = [pl.BlockSpec((tm // d, d * D_ATTN), row) for d in DILATIONS]
    l_specs = [pl.BlockSpec((tm // d, d * LANES), row) for d in DILATIONS]
    return pl.pallas_call(
        _out_proj_kernel,
        grid=(t_total // tm,),
        in_specs=[wide, wide] + o_specs + l_specs
                 + [wide, _const_spec(w_out.shape), _const_spec(norm_f.shape)],
        out_specs=wide,
        out_shape=jax.ShapeDtypeStruct((t_total, D_MODEL), F32),
        scratch_shapes=[pltpu.VMEM((D_ATTN // LANES, tm, LANES), F32),
                        pltpu.VMEM((D_ATTN // LANES, SUB_DIL, tm // SUB_DIL, LANES), F32),
                        pltpu.VMEM((1, tm, LANES), F32),
                        pltpu.VMEM((1, SUB_DIL, tm // SUB_DIL, LANES), F32)],
        compiler_params=_cparams(("arbitrary",)),
        name="out_proj",
    )(x2, y_ssd, *os_, *lses, za, w_out, norm_f)


def _rope_tables(seq_len):
    half = HEAD_DIM // 2
    inv = ROPE_THETA ** (-jnp.arange(half, dtype=F32) * 2.0 / HEAD_DIM)
    ang = jnp.arange(seq_len, dtype=F32)[:, None] * inv[None, :]
    cos, sin = jnp.cos(ang), jnp.sin(ang)
    cos_h = jnp.concatenate([cos, cos], axis=-1)
    sin_h = jnp.concatenate([-sin, sin], axis=-1)
    reps = LANES // HEAD_DIM
    return jnp.tile(cos_h, (1, reps)), jnp.tile(sin_h, (1, reps))


def _pad_lanes(v, offset=0):
    out = jnp.zeros((1, LANES), F32)
    return out.at[0, offset:offset + v.shape[0]].set(v.astype(F32))


def _encoder(x, tables, params):
    (ws, conv_w, conv_b, dtb, alog, dskip_x, norm_ssd, norm_in, w_out, norm_f) = params
    batch, seq_len, _ = x.shape
    t_total = batch * seq_len
    x2 = x.reshape(t_total, D_MODEL)
    cos, sin = tables
    q, k, v, za, zs, xbc, dt = _in_proj(x2, norm_in, cos[:seq_len], sin[:seq_len], ws,
                                        seq_len, tm=IN_PROJ_ROWS)
    y_f, act = _ssd_fwd(xbc, dt, conv_w, conv_b, dtb, alog, batch, seq_len)
    y_ssd = _ssd_bwd(act, dt, dtb, alog, y_f, zs, dskip_x, norm_ssd, batch, seq_len)
    os_, lses = [], []
    for i, dil in enumerate(DILATIONS):
        o, lse = _band_attn(q[i], k[i], v[i], batch, seq_len, dil)
        os_.append(o)
        lses.append(lse)
    y = _out_proj(x2, y_ssd, os_, lses, za, w_out, norm_f, tm=OUT_PROJ_ROWS)
    return y.reshape(batch, seq_len, D_MODEL)


def kernel(x_prompt, x_sample, w_in, conv_w, conv_b, dt_bias_fwd, dt_bias_bwd, a_log_fwd,
           a_log_bwd, d_skip, norm_ssd, norm_in, w_out, norm_f):
    assert w_in.shape[0] == 1, "single-layer encoder"
    w = w_in[0]
    bounds = [0, D_ATTN, 2 * D_ATTN, 3 * D_ATTN, 4 * D_ATTN, 4 * D_ATTN + D_SSD,
              4 * D_ATTN + D_SSD + D_XBC]
    segs = [w[:, bounds[i]:bounds[i + 1]].astype(BF16) for i in range(6)]
    w_dt = jnp.zeros((D_MODEL, LANES), F32).at[:, :2 * N_HEADS].set(w[:, bounds[6]:])
    ws = tuple(segs) + (w_dt.astype(BF16),)
    dtb = _pad_lanes(dt_bias_fwd[0]) + _pad_lanes(dt_bias_bwd[0], N_HEADS)
    alog = _pad_lanes(a_log_fwd[0]) + _pad_lanes(a_log_bwd[0], N_HEADS)
    dskip_x = jnp.repeat(d_skip[0].astype(F32), HEAD_DIM)[None, :]
    params = (ws, conv_w[0].astype(F32), conv_b[0].astype(F32)[None, :], dtb, alog, dskip_x,
              norm_ssd[0].astype(F32)[None, :], norm_in[0].astype(F32)[None, :],
              w_out[0].astype(BF16), norm_f.astype(F32)[None, :])
    tables = _rope_tables(max(x_prompt.shape[1], x_sample.shape[1]))
    return (_encoder(x_prompt, tables, params), _encoder(x_sample, tables, params))
```

```python
import functools
import itertools

import jax
import jax.numpy as jnp
from jax import lax
from jax.experimental import pallas as pl
from jax.experimental.pallas import tpu as pltpu

D_MODEL = 1024
D_ATTN = 1024
D_SSD = 1024
HEAD_DIM = 64
HEAD_SHIFT = 6
N_HEADS = 16
D_STATE = 128
SSD_GROUPS = 2
CONV_WIDTH = 5
CHUNK = 128
ROPE_THETA = 10000.0
EPS = 1e-5
DILATIONS = (1, 4, 16)
SUB_DIL = 4
BAND_RADIUS = 64
D_XBC = D_SSD + 2 * SSD_GROUPS * D_STATE
LANES = 128
HALO_ROWS = 16
NEG_BIG = -1e30
LOG2E = 1.4426950408889634
Q_SCALE = HEAD_DIM ** -0.5 * LOG2E

F32 = jnp.float32
BF16 = jnp.bfloat16

VMEM_LIMIT = 56 * 1024 * 1024
IN_PROJ_ROWS = 512
OUT_PROJ_ROWS = 512
SSD_CHUNKS_PER_STEP = 4


def _cparams(sem):
    return pltpu.CompilerParams(dimension_semantics=sem, vmem_limit_bytes=VMEM_LIMIT)


def _const_spec(shape):
    nd = len(shape)
    return pl.BlockSpec(shape, lambda *_: (0,) * nd)


def _in_proj_kernel(x_ref, g_ref, cos_ref, sin_ref, wq_ref, wk_ref, wv_ref, wza_ref,
                    wzs_ref, wxbc_ref, wdt_ref,
                    q1_ref, q4_ref, q16_ref, k1_ref, k4_ref, k16_ref, v1_ref, v4_ref, v16_ref,
                    za_ref, zs_ref, xbc_ref, dt_ref, blk_scr, sub_scr):
    tm = x_ref.shape[0]

    def emit_dilated(t, j, outs):
        o1_ref, o4_ref, o16_ref = outs
        o1_ref[:, j * LANES:(j + 1) * LANES] = t.astype(BF16)
        blk_scr[...] = t
        for c4 in range(SUB_DIL):
            a = blk_scr[pl.ds(c4, tm // SUB_DIL, stride=SUB_DIL), :]
            col = c4 * D_ATTN + j * LANES
            o4_ref[:, col:col + LANES] = a.astype(BF16)
            sub_scr[c4] = a
        for c4 in range(SUB_DIL):
            for c2 in range(SUB_DIL):
                col = (c4 + SUB_DIL * c2) * D_ATTN + j * LANES
                o16_ref[:, col:col + LANES] = sub_scr[
                    c4, pl.ds(c2, tm // (SUB_DIL * SUB_DIL), stride=SUB_DIL), :].astype(BF16)

    x = x_ref[...]
    ms = jnp.mean(x * x, axis=-1, keepdims=True)
    h = (x * lax.rsqrt(ms + EPS) * g_ref[...]).astype(BF16)

    cos = cos_ref[...]
    sin = sin_ref[...]
    lane = lax.broadcasted_iota(jnp.int32, cos.shape, 1)
    first_half = (lane & (HEAD_DIM - 1)) < (HEAD_DIM // 2)

    def rotary(t):
        partner = jnp.where(first_half,
                            pltpu.roll(t, LANES - HEAD_DIM // 2, 1),
                            pltpu.roll(t, HEAD_DIM // 2, 1))
        return t * cos + partner * sin

    q_outs = (q1_ref, q4_ref, q16_ref)
    k_outs = (k1_ref, k4_ref, k16_ref)
    v_outs = (v1_ref, v4_ref, v16_ref)
    mxu_cols = 2 * LANES
    for jj in range(D_ATTN // mxu_cols):
        sl = slice(jj * mxu_cols, (jj + 1) * mxu_cols)
        tq = jnp.dot(h, wq_ref[:, sl], preferred_element_type=F32)
        tk = jnp.dot(h, wk_ref[:, sl], preferred_element_type=F32)
        tv = jnp.dot(h, wv_ref[:, sl], preferred_element_type=F32)
        for half in range(2):
            hs = slice(half * LANES, (half + 1) * LANES)
            j = 2 * jj + half
            emit_dilated(rotary(tq[:, hs]) * Q_SCALE, j, q_outs)
            emit_dilated(rotary(tk[:, hs]), j, k_outs)
            emit_dilated(tv[:, hs], j, v_outs)

    for w_ref, o_ref in ((wza_ref, za_ref), (wzs_ref, zs_ref),
                         (wxbc_ref, xbc_ref), (wdt_ref, dt_ref)):
        o_ref[...] = jnp.dot(h, w_ref[...], preferred_element_type=F32).astype(o_ref.dtype)


def _in_proj(x2, norm_in, cos, sin, ws, seq_len, tm):
    t_total = x2.shape[0]
    blocks_per_seq = seq_len // tm
    row = lambda i: (i, 0)
    pos = lambda i: (i % blocks_per_seq, 0)
    attn_shapes = [jax.ShapeDtypeStruct((t_total // d, d * D_ATTN), BF16) for d in DILATIONS]
    attn_specs = [pl.BlockSpec((tm // d, d * D_ATTN), row) for d in DILATIONS]
    out_shapes = (attn_shapes * 3
                  + [jax.ShapeDtypeStruct((t_total, D_ATTN), BF16)] * 2
                  + [jax.ShapeDtypeStruct((t_total, D_XBC), BF16),
                     jax.ShapeDtypeStruct((t_total, LANES), F32)])
    weight_specs = [pl.BlockSpec(w.shape, lambda i: (0, 0), pipeline_mode=pl.Buffered(1))
                    for w in ws]
    outs = pl.pallas_call(
        _in_proj_kernel,
        grid=(t_total // tm,),
        in_specs=[pl.BlockSpec((tm, D_MODEL), row), _const_spec((1, D_MODEL)),
                  pl.BlockSpec((tm, LANES), pos), pl.BlockSpec((tm, LANES), pos)]
                 + weight_specs,
        out_specs=attn_specs * 3
                  + [pl.BlockSpec((tm, D_ATTN), row)] * 2
                  + [pl.BlockSpec((tm, D_XBC), row), pl.BlockSpec((tm, LANES), row)],
        out_shape=out_shapes,
        scratch_shapes=[pltpu.VMEM((tm, LANES), F32),
                        pltpu.VMEM((SUB_DIL, tm // SUB_DIL, LANES), F32)],
        compiler_params=_cparams(("arbitrary",)),
        name="in_proj",
    )(x2, norm_in, cos, sin, *ws)
    q, k, v = outs[0:3], outs[3:6], outs[6:9]
    return q, k, v, outs[9], outs[10], outs[11], outs[12]


def _silu(x):
    return x / (1.0 + jnp.exp(-x))


def _softplus(x):
    return jnp.maximum(x, 0.0) + jnp.log(1.0 + jnp.exp(-jnp.abs(x)))


CONV_PAD = (CONV_WIDTH - 1) // 2
CONV_EXT = 2 * CHUNK
CONV_LEAD = (CONV_EXT - CHUNK) // 2
CONV_TAPS = tuple(j for j in range(CONV_WIDTH) if j != CONV_PAD)


def _init_shift_matrix(shift_scr):
    shape = (len(CONV_TAPS) * CHUNK, CONV_EXT)
    row = lax.broadcasted_iota(jnp.int32, shape, 0)
    col = lax.broadcasted_iota(jnp.int32, shape, 1)
    t = row & (CHUNK - 1)
    blk = lax.shift_right_logical(row, CHUNK.bit_length() - 1)
    off = jnp.zeros(shape, jnp.int32)
    for i, j in enumerate(CONV_TAPS):
        off = jnp.where(blk == i, j - CONV_PAD, off)
    shift_scr[...] = jnp.where(col == CONV_LEAD + t + off, 1.0, 0.0).astype(BF16)


def _conv_silu(lo, mid, hi, shift_scr, cw_ref, cb_ref):
    zeros = jnp.zeros((CONV_LEAD - HALO_ROWS, D_XBC), BF16)
    window = jnp.concatenate([zeros, lo, mid, hi, zeros], axis=0)
    shifted = jnp.dot(shift_scr[...], window, preferred_element_type=F32)
    conv = cb_ref[...] + mid.astype(F32) * cw_ref[CONV_PAD:CONV_PAD + 1, :]
    for i, j in enumerate(CONV_TAPS):
        conv = conv + shifted[i * CHUNK:(i + 1) * CHUNK, :] * cw_ref[j:j + 1, :]
    return _silu(conv)


def _ssd_body(refs, reverse, n_steps, cps, batch_id, step):
    if reverse:
        (act_in_ref, dt_ref, dtb_ref, alog_ref, yf_ref, z_ref, dskip_ref, gn_ref,
         y_ref, state_scr) = refs
    else:
        (xbc_ref, prev_ref, next_ref, dt_ref, cw_ref, cb_ref, dtb_ref, alog_ref,
         y_ref, act_out_ref, state_scr, shift_scr) = refs

    @pl.when(step == 0)
    def _():
        state_scr[...] = jnp.zeros_like(state_scr)

    if not reverse:
        @pl.when((batch_id == 0) & (step == 0))
        def _():
            _init_shift_matrix(shift_scr)

        halo_zeros = jnp.zeros((HALO_ROWS, D_XBC), BF16)
        prev = jnp.where(step > 0, prev_ref[...], halo_zeros)
        nxt = jnp.where(step < n_steps - 1, next_ref[...], halo_zeros)

    for ci in range(cps):
        c = (cps - 1 - ci) if reverse else ci
        rows = slice(c * CHUNK, (c + 1) * CHUNK)
        if reverse:
            act = act_in_ref[rows, :].astype(F32)
            extra = (yf_ref[rows, :], z_ref[rows, :], dskip_ref, gn_ref)
        else:
            lo = prev if c == 0 else xbc_ref[c * CHUNK - HALO_ROWS:c * CHUNK, :]
            hi = (nxt if c == cps - 1
                  else xbc_ref[(c + 1) * CHUNK:(c + 1) * CHUNK + HALO_ROWS, :])
            act = _conv_silu(lo, xbc_ref[rows, :], hi, shift_scr, cw_ref, cb_ref)
            act_out_ref[rows, :] = act.astype(act_out_ref.dtype)
            extra = None
        y = _ssd_chunk(act, dt_ref[rows, :], dtb_ref, alog_ref, state_scr, reverse, extra)
        y_ref[rows, :] = y.astype(y_ref.dtype)
        yield


def _ssd_chunk(act, dt_raw, dtb_ref, alog_ref, state_scr, reverse, extra):
    head_off = N_HEADS if reverse else 0
    last = 0 if reverse else CHUNK - 1
    xs = act[:, :D_SSD]
    b_all = act[:, D_SSD:D_SSD + SSD_GROUPS * D_STATE]
    c_all = act[:, D_SSD + SSD_GROUPS * D_STATE:].astype(BF16)

    pre_row = (dt_raw + dtb_ref[...]).T[:2 * N_HEADS]
    dt_row = _softplus(pre_row)
    dt_col = jnp.concatenate(
        [dt_row, jnp.zeros((LANES - 2 * N_HEADS, CHUNK), F32)], axis=0).T
    a_col = dt_col * (-jnp.exp(alog_ref[...]))
    r_io = lax.broadcasted_iota(jnp.int32, (CHUNK, CHUNK), 0)
    c_io = lax.broadcasted_iota(jnp.int32, (CHUNK, CHUNK), 1)
    mask = (c_io >= r_io) if reverse else (c_io <= r_io)
    tri = jnp.where(mask, 1.0, 0.0).astype(BF16)
    acum_col = jnp.zeros((CHUNK, LANES), F32)
    rest = a_col
    for _ in range(3):
        part = rest.astype(BF16)
        acum_col = acum_col + jnp.dot(tri, part, preferred_element_type=F32)
        rest = rest - part.astype(F32)
    acum_row = acum_col.T

    lane = lax.broadcasted_iota(jnp.int32, (LANES, D_SSD), 1)
    rowi = lax.broadcasted_iota(jnp.int32, (LANES, D_SSD), 0)
    expand = jnp.where(rowi == head_off + lax.shift_right_logical(lane, HEAD_SHIFT),
                       1.0, 0.0).astype(BF16)
    w_col = dt_col * jnp.exp(acum_col[last:last + 1, :] - acum_col)
    stacked = jnp.concatenate([jnp.exp(acum_col), w_col], axis=0).astype(BF16)
    expanded = jnp.dot(stacked, expand, preferred_element_type=F32)
    ea_x, w_x = expanded[:CHUNK], expanded[CHUNK:]
    xs_b = xs.astype(BF16)
    xw_b = (xs * w_x).astype(BF16)

    state_in = state_scr[...]
    state_b = state_in.astype(BF16)

    lane2 = lax.broadcasted_iota(jnp.int32, (CHUNK, LANES), 1)
    low_half = lane2 < HEAD_DIM
    zero_b = jnp.zeros((CHUNK, LANES), BF16)

    heads_per_group = N_HEADS // SSD_GROUPS
    gw = heads_per_group * HEAD_DIM
    y_parts = []
    s_parts = []
    for g in range(SSD_GROUPS):
        bg_f = b_all[:, g * D_STATE:(g + 1) * D_STATE]
        bg = bg_f.astype(BF16)
        cg = c_all[:, g * D_STATE:(g + 1) * D_STATE]
        cb = jnp.einsum('ln,sn->ls', cg, bg, preferred_element_type=F32)
        y_off = jnp.dot(cg, state_b[:, g * gw:(g + 1) * gw], preferred_element_type=F32)
        s_parts.append(jnp.dot(bg_f.T.astype(BF16), xw_b[:, g * gw:(g + 1) * gw],
                               preferred_element_type=F32))
        for p in range(heads_per_group // 2):
            h0 = g * heads_per_group + 2 * p
            ms = []
            for hh in (h0, h0 + 1):
                idx = head_off + hh
                seg = acum_col[:, idx:idx + 1] - acum_row[idx:idx + 1, :]
                ms.append(cb * jnp.exp(jnp.where(mask, seg, NEG_BIG)) * dt_row[idx:idx + 1, :])
            lhs = jnp.concatenate(ms, axis=1).astype(BF16)
            xp = xs_b[:, h0 * HEAD_DIM:h0 * HEAD_DIM + LANES]
            rhs = jnp.concatenate([jnp.where(low_half, xp, zero_b),
                                   jnp.where(low_half, zero_b, xp)], axis=0)
            y_diag = jnp.dot(lhs, rhs, preferred_element_type=F32)
            sl = slice(p * LANES, (p + 1) * LANES)
            y_parts.append(y_diag + y_off[:, sl] * ea_x[:, g * gw + p * LANES:g * gw + (p + 1) * LANES])

    state_scr[...] = state_in * ea_x[last:last + 1, :] + jnp.concatenate(s_parts, axis=1)
    y = jnp.concatenate(y_parts, axis=1)

    if reverse:
        yf, z, dskip_ref, gn_ref = extra
        y = y + yf.astype(F32) + dskip_ref[...] * xs
        y = y * _silu(z.astype(F32))
        ms2 = jnp.mean(y * y, axis=-1, keepdims=True)
        y = y * lax.rsqrt(ms2 + EPS) * gn_ref[...]
    return y


SSD_STEP_ROWS = SSD_CHUNKS_PER_STEP * CHUNK


def _ssd_fwd_plan(t_total, seq_len, small_shapes):
    rows = SSD_STEP_ROWS
    n_steps = seq_len // rows
    halo_per_step = rows // HALO_ROWS
    n_halo_blocks = t_total // HALO_ROWS
    sidx = lambda b, i: b * n_steps + i
    main = lambda b, i: (sidx(b, i), 0)
    prev = lambda b, i: (jnp.maximum(sidx(b, i) * halo_per_step - 1, 0), 0)
    nxt = lambda b, i: (jnp.minimum((sidx(b, i) + 1) * halo_per_step, n_halo_blocks - 1), 0)
    return dict(
        in_specs=[pl.BlockSpec((rows, D_XBC), main),
                  pl.BlockSpec((HALO_ROWS, D_XBC), prev),
                  pl.BlockSpec((HALO_ROWS, D_XBC), nxt),
                  pl.BlockSpec((rows, LANES), main)] + [_const_spec(s) for s in small_shapes],
        out_specs=[pl.BlockSpec((rows, D_SSD), main), pl.BlockSpec((rows, D_XBC), main)],
        out_shape=[jax.ShapeDtypeStruct((t_total, D_SSD), BF16),
                   jax.ShapeDtypeStruct((t_total, D_XBC), BF16)],
        scratch=[pltpu.VMEM((D_STATE, D_SSD), F32),
                 pltpu.VMEM((len(CONV_TAPS) * CHUNK, CONV_EXT), BF16)])


def _ssd_bwd_plan(t_total, seq_len, small_shapes):
    rows = SSD_STEP_ROWS
    n_steps = seq_len // rows
    main = lambda b, i: (b * n_steps + n_steps - 1 - i, 0)
    wide = pl.BlockSpec((rows, D_SSD), main)
    dtb_s, alog_s, dskip_s, gn_s = small_shapes
    return dict(
        in_specs=[pl.BlockSpec((rows, D_XBC), main), pl.BlockSpec((rows, LANES), main),
                  _const_spec(dtb_s), _const_spec(alog_s), wide, wide,
                  _const_spec(dskip_s), _const_spec(gn_s)],
        out_specs=[wide],
        out_shape=[jax.ShapeDtypeStruct((t_total, D_SSD), BF16)],
        scratch=[pltpu.VMEM((D_STATE, D_SSD), F32)])


def _ssd_with_attn_kernel(*refs, reverse, n_steps, n_ssd, n_attn, attn_tl, attn_lblocks):
    n_ssd_in, n_ssd_out, n_ssd_scr = n_ssd
    n_attn_in, n_attn_out, n_attn_scr = n_attn
    it = iter(refs)
    take = lambda n: [next(it) for _ in range(n)]
    ssd_in, attn_in = take(n_ssd_in), take(n_attn_in)
    ssd_out, attn_out = take(n_ssd_out), take(n_attn_out)
    ssd_scr, attn_scr = take(n_ssd_scr), take(n_attn_scr)
    batch_id, step = pl.program_id(0), pl.program_id(1)
    _band_attn_init(attn_scr[-1], (batch_id == 0) & (step == 0))
    ssd_steps = _ssd_body(ssd_in + ssd_out + ssd_scr, reverse, n_steps, SSD_CHUNKS_PER_STEP,
                          batch_id, step)
    attn_steps = _band_attn_main(attn_in + attn_out + attn_scr, attn_tl, attn_lblocks,
                                 step % attn_lblocks)
    for _ in itertools.chain(ssd_steps, attn_steps):
        pass


def _ssd_with_attn(ssd_plan, ssd_args, reverse, qd, kd, vd, batch, seq_len, dil):
    n_steps = seq_len // SSD_STEP_ROWS
    probe = _attn_plan(qd.shape[0], seq_len, dil, lambda b, i: (b, 0, 0))
    nl = probe["n_lblocks"]
    assert dil * nl == n_steps, "attention blocks must tile the SSD grid one to one"
    attn = _attn_plan(qd.shape[0], seq_len, dil, lambda b, i: (b, i // nl, i % nl))
    outs = pl.pallas_call(
        functools.partial(
            _ssd_with_attn_kernel, reverse=reverse, n_steps=n_steps,
            n_ssd=(len(ssd_plan["in_specs"]), len(ssd_plan["out_specs"]), len(ssd_plan["scratch"])),
            n_attn=(len(attn["in_specs"]), len(attn["out_specs"]), len(attn["scratch"])),
            attn_tl=attn["tl"], attn_lblocks=nl),
        grid=(batch, n_steps),
        in_specs=ssd_plan["in_specs"] + attn["in_specs"],
        out_specs=ssd_plan["out_specs"] + attn["out_specs"],
        out_shape=ssd_plan["out_shape"] + attn["out_shape"],
        scratch_shapes=ssd_plan["scratch"] + attn["scratch"],
        compiler_params=_cparams(("arbitrary", "arbitrary")),
        name=("ssd_bwd" if reverse else "ssd_fwd") + f"_attn_d{dil}",
    )(*ssd_args, qd, kd, kd, kd, vd, vd, vd)
    n_ssd_out = len(ssd_plan["out_specs"])
    return outs[:n_ssd_out], outs[n_ssd_out:]


Q_SUB = 128
K_WIN = Q_SUB + 2 * BAND_RADIUS


def _band_attn_kernel(*refs, tl, n_lblocks):
    first = (pl.program_id(0) == 0) & (pl.program_id(1) == 0) & (pl.program_id(2) == 0)
    _band_attn_body(refs, tl, n_lblocks, first, pl.program_id(2))


def _band_attn_body(refs, tl, n_lblocks, first_step, lb):
    _band_attn_init(refs[-1], first_step)
    for _ in _band_attn_main(refs, tl, n_lblocks, lb):
        pass


def _band_attn_init(bias_scr, first_step):
    r = BAND_RADIUS

    @pl.when(first_step)
    def _():
        t_io = lax.broadcasted_iota(jnp.int32, (2 * Q_SUB, K_WIN), 0) & (Q_SUB - 1)
        u_io = lax.broadcasted_iota(jnp.int32, (2 * Q_SUB, K_WIN), 1)
        band = (u_io >= t_io) & (u_io <= t_io + 2 * r)
        bias_scr[0] = jnp.where(band, 0.0, NEG_BIG)
        bias_scr[1] = jnp.where(band & (u_io >= r), 0.0, NEG_BIG)
        bias_scr[2] = jnp.where(band & (u_io < K_WIN - r), 0.0, NEG_BIG)


def _band_attn_main(refs, tl, n_lblocks, lb):
    (q_ref, k_ref, kp_ref, kn_ref, v_ref, vp_ref, vn_ref,
     o_ref, stat_ref, kbuf, vbuf, bias_scr) = refs
    r = BAND_RADIUS
    kbuf[0:r, :] = kp_ref[...]
    kbuf[r:r + tl, :] = k_ref[...]
    kbuf[r + tl:, :] = kn_ref[...]
    vbuf[0:r, :] = vp_ref[...]
    vbuf[r:r + tl, :] = v_ref[...]
    vbuf[r + tl:, :] = vn_ref[...]

    lane = lax.broadcasted_iota(jnp.int32, (Q_SUB, LANES), 1)
    low_half = lane < HEAD_DIM
    zero_b = jnp.zeros((Q_SUB, LANES), BF16)

    n_sub = tl // Q_SUB
    assert n_sub >= 2, "first and last score tiles of a block must differ"
    for j in range(n_sub):
        if j == 0:
            slot = jnp.where(lb == 0, 1, 0)
        elif j == n_sub - 1:
            slot = jnp.where(lb == n_lblocks - 1, 2, 0)
        else:
            slot = 0
        stat_tile = jnp.zeros((Q_SUB, LANES), F32)
        rows = slice(j * Q_SUB, (j + 1) * Q_SUB)
        for p in range(N_HEADS // 2):
            cols = slice(p * LANES, (p + 1) * LANES)
            qp = q_ref[rows, cols]
            qs = jnp.concatenate([jnp.where(low_half, qp, zero_b),
                                  jnp.where(low_half, zero_b, qp)], axis=0)
            kw = kbuf[j * Q_SUB:j * Q_SUB + K_WIN, cols]
            vw = vbuf[j * Q_SUB:j * Q_SUB + K_WIN, cols]
            s = jnp.einsum('qd,kd->qk', qs, kw, preferred_element_type=F32) + bias_scr[slot]
            m = jnp.max(s, axis=-1, keepdims=True)
            e = jnp.exp2(s - m)
            l = jnp.sum(e, axis=-1, keepdims=True)
            o2 = jnp.dot(e.astype(BF16), vw, preferred_element_type=F32)
            o_ref[rows, cols] = jnp.where(low_half, o2[:Q_SUB], o2[Q_SUB:]).astype(o_ref.dtype)
            for hh, sl in ((2 * p, slice(0, Q_SUB)), (2 * p + 1, slice(Q_SUB, 2 * Q_SUB))):
                stat_tile = jnp.where(lane == hh, m[sl], stat_tile)
                stat_tile = jnp.where(lane == N_HEADS + hh, l[sl], stat_tile)
        stat_ref[rows, :] = stat_tile
        yield


def _band_attn(qd, kd, vd, batch, seq_len, dil):
    plan = _attn_plan(qd.shape[0], seq_len, dil, lambda b, c, l: (b, c, l))
    return pl.pallas_call(
        functools.partial(_band_attn_kernel, tl=plan["tl"], n_lblocks=plan["n_lblocks"]),
        grid=(batch, dil, plan["n_lblocks"]),
        in_specs=plan["in_specs"],
        out_specs=plan["out_specs"],
        out_shape=plan["out_shape"],
        scratch_shapes=plan["scratch"],
        compiler_params=_cparams(("arbitrary", "arbitrary", "arbitrary")),
        name=f"band_attn_d{dil}",
    )(qd, kd, kd, kd, vd, vd, vd)


def _attn_plan(n_rows, seq_len, dil, to_bcl):
    sub_len = seq_len // dil
    tl = min(512, sub_len)
    n_lblocks = sub_len // tl
    r = BAND_RADIUS
    halo_per_block = tl // r
    n_halo = n_rows // r

    def main(*ids):
        b, c, l = to_bcl(*ids)
        return (b * n_lblocks + l, c)

    def prev(*ids):
        b, c, l = to_bcl(*ids)
        return (jnp.maximum((b * n_lblocks + l) * halo_per_block - 1, 0), c)

    def nxt(*ids):
        b, c, l = to_bcl(*ids)
        return (jnp.minimum((b * n_lblocks + l + 1) * halo_per_block, n_halo - 1), c)

    blk = pl.BlockSpec((tl, D_ATTN), main)
    hp = pl.BlockSpec((r, D_ATTN), prev)
    hn = pl.BlockSpec((r, D_ATTN), nxt)
    return dict(
        tl=tl, n_lblocks=n_lblocks,
        in_specs=[blk, blk, hp, hn, blk, hp, hn],
        out_specs=[blk, pl.BlockSpec((tl, LANES), main)],
        out_shape=[jax.ShapeDtypeStruct((n_rows, dil * D_ATTN), BF16),
                   jax.ShapeDtypeStruct((n_rows, dil * LANES), F32)],
        scratch=[pltpu.VMEM((tl + 2 * r, D_ATTN), BF16),
                 pltpu.VMEM((tl + 2 * r, D_ATTN), BF16),
                 pltpu.VMEM((3, 2 * Q_SUB, K_WIN), F32)])


def _out_proj_kernel(x_ref, ys_ref, o1_ref, o2_ref, o3_ref, l1_ref, l2_ref, l3_ref,
                     za_ref, wo_ref, gf_ref, y_ref, o_scr, omid_scr, l_scr, lmid_scr):
    tm = x_ref.shape[0]

    def token_order(src_ref, scr, mid, dil, width):
        for j in range(width // LANES):
            if dil == SUB_DIL:
                for c in range(dil):
                    col = c * width + j * LANES
                    scr[j, pl.ds(c, tm // dil, stride=dil), :] = (
                        src_ref[:, col:col + LANES].astype(F32))
            else:
                for c4 in range(SUB_DIL):
                    for c2 in range(SUB_DIL):
                        col = (c4 + SUB_DIL * c2) * width + j * LANES
                        mid[j, c4, pl.ds(c2, tm // dil, stride=SUB_DIL), :] = (
                            src_ref[:, col:col + LANES].astype(F32))
                for c4 in range(SUB_DIL):
                    scr[j, pl.ds(c4, tm // SUB_DIL, stride=SUB_DIL), :] = mid[j, c4]
        return jnp.concatenate([scr[j] for j in range(width // LANES)], axis=1)

    s1 = l1_ref[...]
    s2 = token_order(l2_ref, l_scr, lmid_scr, DILATIONS[1], LANES)
    s3 = token_order(l3_ref, l_scr, lmid_scr, DILATIONS[2], LANES)
    m = jnp.maximum(jnp.maximum(s1, s2), s3)
    head_lane = lax.broadcasted_iota(jnp.int32, s1.shape, 1) < N_HEADS
    a = [jnp.exp2(s - m) for s in (s1, s2, s3)]
    den = sum(ap * pltpu.roll(s, LANES - N_HEADS, 1) for ap, s in zip(a, (s1, s2, s3)))
    inv = jnp.where(head_lane, 1.0 / jnp.where(head_lane, den, 1.0), 0.0)
    lane = lax.broadcasted_iota(jnp.int32, (LANES, D_ATTN), 1)
    rowi = lax.broadcasted_iota(jnp.int32, (LANES, D_ATTN), 0)
    expand = jnp.where(rowi == lax.shift_right_logical(lane, HEAD_SHIFT), 1.0, 0.0).astype(BF16)

    def weight(ap):
        return jnp.dot((ap * inv).astype(BF16), expand, preferred_element_type=F32)

    mix = weight(a[0]) * o1_ref[...].astype(F32)
    mix = mix + weight(a[1]) * token_order(o2_ref, o_scr, omid_scr, DILATIONS[1], D_ATTN)
    mix = mix + weight(a[2]) * token_order(o3_ref, o_scr, omid_scr, DILATIONS[2], D_ATTN)
    z = za_ref[...].astype(F32)
    y_attn = (mix * (z / (1.0 + jnp.exp(-z)))).astype(BF16)
    out = x_ref[...]
    out = out + jnp.dot(ys_ref[...], wo_ref[:D_SSD, :], preferred_element_type=F32)
    out = out + jnp.dot(y_attn, wo_ref[D_SSD:, :], preferred_element_type=F32)
    ms = jnp.mean(out * out, axis=-1, keepdims=True)
    y_ref[...] = out * lax.rsqrt(ms + EPS) * gf_ref[...]


def _out_proj(x2, y_ssd, os_, lses, za, w_out, norm_f, tm):
    t_total = x2.shape[0]
    row = lambda i: (i, 0)
    wide = pl.BlockSpec((tm, D_MODEL), row)
    o_specs = [pl.BlockSpec((tm // d, d * D_ATTN), row) for d in DILATIONS]
    l_specs = [pl.BlockSpec((tm // d, d * LANES), row) for d in DILATIONS]
    return pl.pallas_call(
        _out_proj_kernel,
        grid=(t_total // tm,),
        in_specs=[wide, wide] + o_specs + l_specs
                 + [wide, _const_spec(w_out.shape), _const_spec(norm_f.shape)],
        out_specs=wide,
        out_shape=jax.ShapeDtypeStruct((t_total, D_MODEL), F32),
        scratch_shapes=[pltpu.VMEM((D_ATTN // LANES, tm, LANES), F32),
                        pltpu.VMEM((D_ATTN // LANES, SUB_DIL, tm // SUB_DIL, LANES), F32),
                        pltpu.VMEM((1, tm, LANES), F32),
                        pltpu.VMEM((1, SUB_DIL, tm // SUB_DIL, LANES), F32)],
        compiler_params=_cparams(("arbitrary",)),
        name="out_proj",
    )(x2, y_ssd, *os_, *lses, za, w_out, norm_f)


def _rope_tables(seq_len):
    half = HEAD_DIM // 2
    inv = ROPE_THETA ** (-jnp.arange(half, dtype=F32) * 2.0 / HEAD_DIM)
    ang = jnp.arange(seq_len, dtype=F32)[:, None] * inv[None, :]
    cos, sin = jnp.cos(ang), jnp.sin(ang)
    cos_h = jnp.concatenate([cos, cos], axis=-1)
    sin_h = jnp.concatenate([-sin, sin], axis=-1)
    reps = LANES // HEAD_DIM
    return jnp.tile(cos_h, (1, reps)), jnp.tile(sin_h, (1, reps))


def _pad_lanes(v, offset=0):
    out = jnp.zeros((1, LANES), F32)
    return out.at[0, offset:offset + v.shape[0]].set(v.astype(F32))


def _encoder(x, tables, params):
    (ws, conv_w, conv_b, dtb, alog, dskip_x, norm_ssd, norm_in, w_out, norm_f) = params
    batch, seq_len, _ = x.shape
    t_total = batch * seq_len
    x2 = x.reshape(t_total, D_MODEL)
    cos, sin = tables
    q, k, v, za, zs, xbc, dt = _in_proj(x2, norm_in, cos[:seq_len], sin[:seq_len], ws,
                                        seq_len, tm=IN_PROJ_ROWS)
    fwd_plan = _ssd_fwd_plan(t_total, seq_len,
                             (conv_w.shape, conv_b.shape, dtb.shape, alog.shape))
    (y_f, act), attn4 = _ssd_with_attn(
        fwd_plan, (xbc, xbc, xbc, dt, conv_w, conv_b, dtb, alog), False,
        q[1], k[1], v[1], batch, seq_len, DILATIONS[1])
    bwd_plan = _ssd_bwd_plan(t_total, seq_len,
                             (dtb.shape, alog.shape, dskip_x.shape, norm_ssd.shape))
    (y_ssd,), attn1 = _ssd_with_attn(
        bwd_plan, (act, dt, dtb, alog, y_f, zs, dskip_x, norm_ssd), True,
        q[0], k[0], v[0], batch, seq_len, DILATIONS[0])
    attn16 = _band_attn(q[2], k[2], v[2], batch, seq_len, DILATIONS[2])
    os_, stats = zip(attn1, attn4, attn16)
    y = _out_proj(x2, y_ssd, os_, stats, za, w_out, norm_f, tm=OUT_PROJ_ROWS)
    return y.reshape(batch, seq_len, D_MODEL)


def kernel(x_prompt, x_sample, w_in, conv_w, conv_b, dt_bias_fwd, dt_bias_bwd, a_log_fwd,
           a_log_bwd, d_skip, norm_ssd, norm_in, w_out, norm_f):
    assert w_in.shape[0] == 1, "single-layer encoder"
    w = w_in[0]
    bounds = [0, D_ATTN, 2 * D_ATTN, 3 * D_ATTN, 4 * D_ATTN, 4 * D_ATTN + D_SSD,
              4 * D_ATTN + D_SSD + D_XBC]
    segs = [w[:, bounds[i]:bounds[i + 1]].astype(BF16) for i in range(6)]
    w_dt = jnp.zeros((D_MODEL, LANES), F32).at[:, :2 * N_HEADS].set(w[:, bounds[6]:])
    ws = tuple(segs) + (w_dt.astype(BF16),)
    dtb = _pad_lanes(dt_bias_fwd[0]) + _pad_lanes(dt_bias_bwd[0], N_HEADS)
    alog = _pad_lanes(a_log_fwd[0]) + _pad_lanes(a_log_bwd[0], N_HEADS)
    dskip_x = jnp.repeat(d_skip[0].astype(F32), HEAD_DIM)[None, :]
    params = (ws, conv_w[0].astype(F32), conv_b[0].astype(F32)[None, :], dtb, alog, dskip_x,
              norm_ssd[0].astype(F32)[None, :], norm_in[0].astype(F32)[None, :],
              w_out[0].astype(BF16), norm_f.astype(F32)[None, :])
    tables = _rope_tables(max(x_prompt.shape[1], x_sample.shape[1]))
    return (_encoder(x_prompt, tables, params), _encoder(x_sample, tables, params))
```

```python
import functools
import itertools

import jax
import jax.numpy as jnp
from jax import lax
from jax.experimental import pallas as pl
from jax.experimental.pallas import tpu as pltpu

D_MODEL = 1024
D_ATTN = 1024
D_SSD = 1024
HEAD_DIM = 64
HEAD_SHIFT = 6
N_HEADS = 16
D_STATE = 128
SSD_GROUPS = 2
CONV_WIDTH = 5
CHUNK = 128
ROPE_THETA = 10000.0
EPS = 1e-5
DILATIONS = (1, 4, 16)
SUB_DIL = 4
BAND_RADIUS = 64
D_XBC = D_SSD + 2 * SSD_GROUPS * D_STATE
LANES = 128
HALO_ROWS = 16
NEG_BIG = -1e30
LOG2E = 1.4426950408889634
Q_SCALE = HEAD_DIM ** -0.5 * LOG2E

F32 = jnp.float32
BF16 = jnp.bfloat16

VMEM_LIMIT = 56 * 1024 * 1024
IN_PROJ_ROWS = 512
OUT_PROJ_ROWS = 512
SSD_CHUNKS_PER_STEP = 4


def _cparams(sem):
    return pltpu.CompilerParams(dimension_semantics=sem, vmem_limit_bytes=VMEM_LIMIT)


def _const_spec(shape):
    nd = len(shape)
    return pl.BlockSpec(shape, lambda *_: (0,) * nd)


def _in_proj_kernel(x_ref, g_ref, cos_ref, sin_ref, wq_ref, wk_ref, wv_ref, wza_ref,
                    wzs_ref, wxbc_ref, wdt_ref,
                    q1_ref, q4_ref, q16_ref, k1_ref, k4_ref, k16_ref, v1_ref, v4_ref, v16_ref,
                    za_ref, zs_ref, xbc_ref, dt_ref, blk_scr, sub_scr):
    tm = x_ref.shape[0]

    def emit_dilated(t, j, outs):
        o1_ref, o4_ref, o16_ref = outs
        o1_ref[:, j * LANES:(j + 1) * LANES] = t.astype(BF16)
        blk_scr[...] = t
        for c4 in range(SUB_DIL):
            a = blk_scr[pl.ds(c4, tm // SUB_DIL, stride=SUB_DIL), :]
            col = c4 * D_ATTN + j * LANES
            o4_ref[:, col:col + LANES] = a.astype(BF16)
            sub_scr[c4] = a
        for c4 in range(SUB_DIL):
            for c2 in range(SUB_DIL):
                col = (c4 + SUB_DIL * c2) * D_ATTN + j * LANES
                o16_ref[:, col:col + LANES] = sub_scr[
                    c4, pl.ds(c2, tm // (SUB_DIL * SUB_DIL), stride=SUB_DIL), :].astype(BF16)

    x = x_ref[...]
    ms = jnp.mean(x * x, axis=-1, keepdims=True)
    h = (x * lax.rsqrt(ms + EPS) * g_ref[...]).astype(BF16)

    cos = cos_ref[...]
    sin = sin_ref[...]
    lane = lax.broadcasted_iota(jnp.int32, cos.shape, 1)
    first_half = (lane & (HEAD_DIM - 1)) < (HEAD_DIM // 2)

    def rotary(t):
        partner = jnp.where(first_half,
                            pltpu.roll(t, LANES - HEAD_DIM // 2, 1),
                            pltpu.roll(t, HEAD_DIM // 2, 1))
        return t * cos + partner * sin

    q_outs = (q1_ref, q4_ref, q16_ref)
    k_outs = (k1_ref, k4_ref, k16_ref)
    v_outs = (v1_ref, v4_ref, v16_ref)
    mxu_cols = 2 * LANES
    for jj in range(D_ATTN // mxu_cols):
        sl = slice(jj * mxu_cols, (jj + 1) * mxu_cols)
        tq = jnp.dot(h, wq_ref[:, sl], preferred_element_type=F32)
        tk = jnp.dot(h, wk_ref[:, sl], preferred_element_type=F32)
        tv = jnp.dot(h, wv_ref[:, sl], preferred_element_type=F32)
        for half in range(2):
            hs = slice(half * LANES, (half + 1) * LANES)
            j = 2 * jj + half
            emit_dilated(rotary(tq[:, hs]) * Q_SCALE, j, q_outs)
            emit_dilated(rotary(tk[:, hs]), j, k_outs)
            emit_dilated(tv[:, hs], j, v_outs)

    for w_ref, o_ref in ((wza_ref, za_ref), (wzs_ref, zs_ref),
                         (wxbc_ref, xbc_ref), (wdt_ref, dt_ref)):
        o_ref[...] = jnp.dot(h, w_ref[...], preferred_element_type=F32).astype(o_ref.dtype)


def _in_proj(x2, norm_in, cos, sin, ws, seq_len, tm):
    t_total = x2.shape[0]
    blocks_per_seq = seq_len // tm
    row = lambda i: (i, 0)
    pos = lambda i: (i % blocks_per_seq, 0)
    attn_shapes = [jax.ShapeDtypeStruct((t_total // d, d * D_ATTN), BF16) for d in DILATIONS]
    attn_specs = [pl.BlockSpec((tm // d, d * D_ATTN), row) for d in DILATIONS]
    out_shapes = (attn_shapes * 3
                  + [jax.ShapeDtypeStruct((t_total, D_ATTN), BF16)] * 2
                  + [jax.ShapeDtypeStruct((t_total, D_XBC), BF16),
                     jax.ShapeDtypeStruct((t_total, LANES), F32)])
    weight_specs = [pl.BlockSpec(w.shape, lambda i: (0, 0), pipeline_mode=pl.Buffered(1))
                    for w in ws]
    outs = pl.pallas_call(
        _in_proj_kernel,
        grid=(t_total // tm,),
        in_specs=[pl.BlockSpec((tm, D_MODEL), row), _const_spec((1, D_MODEL)),
                  pl.BlockSpec((tm, LANES), pos), pl.BlockSpec((tm, LANES), pos)]
                 + weight_specs,
        out_specs=attn_specs * 3
                  + [pl.BlockSpec((tm, D_ATTN), row)] * 2
                  + [pl.BlockSpec((tm, D_XBC), row), pl.BlockSpec((tm, LANES), row)],
        out_shape=out_shapes,
        scratch_shapes=[pltpu.VMEM((tm, LANES), F32),
                        pltpu.VMEM((SUB_DIL, tm // SUB_DIL, LANES), F32)],
        compiler_params=_cparams(("arbitrary",)),
        name="in_proj",
    )(x2, norm_in, cos, sin, *ws)
    q, k, v = outs[0:3], outs[3:6], outs[6:9]
    return q, k, v, outs[9], outs[10], outs[11], outs[12]


def _silu(x):
    return x / (1.0 + jnp.exp(-x))


def _softplus(x):
    return jnp.maximum(x, 0.0) + jnp.log(1.0 + jnp.exp(-jnp.abs(x)))


CONV_PAD = (CONV_WIDTH - 1) // 2
CONV_EXT = 2 * CHUNK
CONV_LEAD = (CONV_EXT - CHUNK) // 2
CONV_TAPS = tuple(j for j in range(CONV_WIDTH) if j != CONV_PAD)


def _init_shift_matrix(shift_scr):
    shape = (len(CONV_TAPS) * CHUNK, CONV_EXT)
    row = lax.broadcasted_iota(jnp.int32, shape, 0)
    col = lax.broadcasted_iota(jnp.int32, shape, 1)
    t = row & (CHUNK - 1)
    blk = lax.shift_right_logical(row, CHUNK.bit_length() - 1)
    off = jnp.zeros(shape, jnp.int32)
    for i, j in enumerate(CONV_TAPS):
        off = jnp.where(blk == i, j - CONV_PAD, off)
    shift_scr[...] = jnp.where(col == CONV_LEAD + t + off, 1.0, 0.0).astype(BF16)


def _conv_silu(lo, mid, hi, shift_scr, cw_ref, cb_ref):
    zeros = jnp.zeros((CONV_LEAD - HALO_ROWS, D_XBC), BF16)
    window = jnp.concatenate([zeros, lo, mid, hi, zeros], axis=0)
    shifted = jnp.dot(shift_scr[...], window, preferred_element_type=F32)
    conv = cb_ref[...] + mid.astype(F32) * cw_ref[CONV_PAD:CONV_PAD + 1, :]
    for i, j in enumerate(CONV_TAPS):
        conv = conv + shifted[i * CHUNK:(i + 1) * CHUNK, :] * cw_ref[j:j + 1, :]
    return _silu(conv)


def _ssd_body(refs, reverse, n_steps, cps, batch_id, step):
    if reverse:
        (act_in_ref, dt_ref, dtb_ref, alog_ref, yf_ref, z_ref, dskip_ref, gn_ref,
         y_ref, state_scr) = refs
    else:
        (xbc_ref, prev_ref, next_ref, dt_ref, cw_ref, cb_ref, dtb_ref, alog_ref,
         y_ref, act_out_ref, state_scr, shift_scr) = refs

    @pl.when(step == 0)
    def _():
        state_scr[...] = jnp.zeros_like(state_scr)

    if not reverse:
        @pl.when((batch_id == 0) & (step == 0))
        def _():
            _init_shift_matrix(shift_scr)

        halo_zeros = jnp.zeros((HALO_ROWS, D_XBC), BF16)
        prev = jnp.where(step > 0, prev_ref[...], halo_zeros)
        nxt = jnp.where(step < n_steps - 1, next_ref[...], halo_zeros)

    for ci in range(cps):
        c = (cps - 1 - ci) if reverse else ci
        rows = slice(c * CHUNK, (c + 1) * CHUNK)
        if reverse:
            act = act_in_ref[rows, :].astype(F32)
            extra = (yf_ref[rows, :], z_ref[rows, :], dskip_ref, gn_ref)
        else:
            lo = prev if c == 0 else xbc_ref[c * CHUNK - HALO_ROWS:c * CHUNK, :]
            hi = (nxt if c == cps - 1
                  else xbc_ref[(c + 1) * CHUNK:(c + 1) * CHUNK + HALO_ROWS, :])
            act = _conv_silu(lo, xbc_ref[rows, :], hi, shift_scr, cw_ref, cb_ref)
            act_out_ref[rows, :] = act.astype(act_out_ref.dtype)
            extra = None
        y = _ssd_chunk(act, dt_ref[rows, :], dtb_ref, alog_ref, state_scr, reverse, extra)
        y_ref[rows, :] = y.astype(y_ref.dtype)
        yield


def _ssd_chunk(act, dt_raw, dtb_ref, alog_ref, state_scr, reverse, extra):
    head_off = N_HEADS if reverse else 0
    last = 0 if reverse else CHUNK - 1
    xs = act[:, :D_SSD]
    b_all = act[:, D_SSD:D_SSD + SSD_GROUPS * D_STATE]
    c_all = act[:, D_SSD + SSD_GROUPS * D_STATE:].astype(BF16)

    pre_row = (dt_raw + dtb_ref[...]).T[:2 * N_HEADS]
    dt_row = _softplus(pre_row)
    dt_col = jnp.concatenate(
        [dt_row, jnp.zeros((LANES - 2 * N_HEADS, CHUNK), F32)], axis=0).T
    a_col = dt_col * (-jnp.exp(alog_ref[...]))
    r_io = lax.broadcasted_iota(jnp.int32, (CHUNK, CHUNK), 0)
    c_io = lax.broadcasted_iota(jnp.int32, (CHUNK, CHUNK), 1)
    mask = (c_io >= r_io) if reverse else (c_io <= r_io)
    tri = jnp.where(mask, 1.0, 0.0).astype(BF16)
    acum_col = jnp.zeros((CHUNK, LANES), F32)
    rest = a_col
    for _ in range(3):
        part = rest.astype(BF16)
        acum_col = acum_col + jnp.dot(tri, part, preferred_element_type=F32)
        rest = rest - part.astype(F32)
    acum2_col = acum_col * LOG2E
    acum2_row = acum2_col.T
    dt_row_b = dt_row.astype(BF16)

    lane = lax.broadcasted_iota(jnp.int32, (LANES, D_SSD), 1)
    rowi = lax.broadcasted_iota(jnp.int32, (LANES, D_SSD), 0)
    expand = jnp.where(rowi == head_off + lax.shift_right_logical(lane, HEAD_SHIFT),
                       1.0, 0.0).astype(BF16)
    w_col = dt_col * jnp.exp(acum_col[last:last + 1, :] - acum_col)
    stacked = jnp.concatenate([jnp.exp(acum_col), w_col], axis=0).astype(BF16)
    expanded = jnp.dot(stacked, expand, preferred_element_type=F32)
    ea_x, w_x = expanded[:CHUNK], expanded[CHUNK:]
    xs_b = xs.astype(BF16)
    xw_b = (xs * w_x).astype(BF16)

    state_in = state_scr[...]
    state_b = state_in.astype(BF16)

    lane2 = lax.broadcasted_iota(jnp.int32, (CHUNK, LANES), 1)
    low_half = lane2 < HEAD_DIM
    zero_b = jnp.zeros((CHUNK, LANES), BF16)

    heads_per_group = N_HEADS // SSD_GROUPS
    gw = heads_per_group * HEAD_DIM
    y_parts = []
    s_parts = []
    for g in range(SSD_GROUPS):
        bg_f = b_all[:, g * D_STATE:(g + 1) * D_STATE]
        bg = bg_f.astype(BF16)
        cg = c_all[:, g * D_STATE:(g + 1) * D_STATE]
        cb = jnp.einsum('ln,sn->ls', cg, bg, preferred_element_type=F32).astype(BF16)
        y_off = jnp.dot(cg, state_b[:, g * gw:(g + 1) * gw], preferred_element_type=F32)
        s_parts.append(jnp.dot(bg_f.T.astype(BF16), xw_b[:, g * gw:(g + 1) * gw],
                               preferred_element_type=F32))
        for p in range(heads_per_group // 2):
            h0 = g * heads_per_group + 2 * p
            ms = []
            for hh in (h0, h0 + 1):
                idx = head_off + hh
                seg = acum2_col[:, idx:idx + 1] - acum2_row[idx:idx + 1, :]
                decay = jnp.exp2(jnp.where(mask, seg, NEG_BIG)).astype(BF16)
                ms.append(decay * cb * dt_row_b[idx:idx + 1, :])
            lhs = jnp.concatenate(ms, axis=1)
            xp = xs_b[:, h0 * HEAD_DIM:h0 * HEAD_DIM + LANES]
            rhs = jnp.concatenate([jnp.where(low_half, xp, zero_b),
                                   jnp.where(low_half, zero_b, xp)], axis=0)
            y_diag = jnp.dot(lhs, rhs, preferred_element_type=F32)
            sl = slice(p * LANES, (p + 1) * LANES)
            y_parts.append(y_diag + y_off[:, sl] * ea_x[:, g * gw + p * LANES:g * gw + (p + 1) * LANES])

    state_scr[...] = state_in * ea_x[last:last + 1, :] + jnp.concatenate(s_parts, axis=1)
    y = jnp.concatenate(y_parts, axis=1)

    if reverse:
        yf, z, dskip_ref, gn_ref = extra
        y = y + yf.astype(F32) + dskip_ref[...] * xs
        y = y * _silu(z.astype(F32))
        ms2 = jnp.mean(y * y, axis=-1, keepdims=True)
        y = y * lax.rsqrt(ms2 + EPS) * gn_ref[...]
    return y


SSD_STEP_ROWS = SSD_CHUNKS_PER_STEP * CHUNK


def _ssd_fwd_plan(t_total, seq_len, small_shapes):
    rows = SSD_STEP_ROWS
    n_steps = seq_len // rows
    halo_per_step = rows // HALO_ROWS
    n_halo_blocks = t_total // HALO_ROWS
    sidx = lambda b, i: b * n_steps + i
    main = lambda b, i: (sidx(b, i), 0)
    prev = lambda b, i: (jnp.maximum(sidx(b, i) * halo_per_step - 1, 0), 0)
    nxt = lambda b, i: (jnp.minimum((sidx(b, i) + 1) * halo_per_step, n_halo_blocks - 1), 0)
    return dict(
        in_specs=[pl.BlockSpec((rows, D_XBC), main),
                  pl.BlockSpec((HALO_ROWS, D_XBC), prev),
                  pl.BlockSpec((HALO_ROWS, D_XBC), nxt),
                  pl.BlockSpec((rows, LANES), main)] + [_const_spec(s) for s in small_shapes],
        out_specs=[pl.BlockSpec((rows, D_SSD), main), pl.BlockSpec((rows, D_XBC), main)],
        out_shape=[jax.ShapeDtypeStruct((t_total, D_SSD), BF16),
                   jax.ShapeDtypeStruct((t_total, D_XBC), BF16)],
        scratch=[pltpu.VMEM((D_STATE, D_SSD), F32),
                 pltpu.VMEM((len(CONV_TAPS) * CHUNK, CONV_EXT), BF16)])


def _ssd_bwd_plan(t_total, seq_len, small_shapes):
    rows = SSD_STEP_ROWS
    n_steps = seq_len // rows
    main = lambda b, i: (b * n_steps + n_steps - 1 - i, 0)
    wide = pl.BlockSpec((rows, D_SSD), main)
    dtb_s, alog_s, dskip_s, gn_s = small_shapes
    return dict(
        in_specs=[pl.BlockSpec((rows, D_XBC), main), pl.BlockSpec((rows, LANES), main),
                  _const_spec(dtb_s), _const_spec(alog_s), wide, wide,
                  _const_spec(dskip_s), _const_spec(gn_s)],
        out_specs=[wide],
        out_shape=[jax.ShapeDtypeStruct((t_total, D_SSD), BF16)],
        scratch=[pltpu.VMEM((D_STATE, D_SSD), F32)])


def _ssd_with_attn_kernel(*refs, reverse, n_steps, n_ssd, n_attn, attn_tl, attn_lblocks):
    n_ssd_in, n_ssd_out, n_ssd_scr = n_ssd
    n_attn_in, n_attn_out, n_attn_scr = n_attn
    it = iter(refs)
    take = lambda n: [next(it) for _ in range(n)]
    ssd_in, attn_in = take(n_ssd_in), take(n_attn_in)
    ssd_out, attn_out = take(n_ssd_out), take(n_attn_out)
    ssd_scr, attn_scr = take(n_ssd_scr), take(n_attn_scr)
    batch_id, step = pl.program_id(0), pl.program_id(1)
    _band_attn_init(attn_scr[-1], (batch_id == 0) & (step == 0))
    ssd_steps = _ssd_body(ssd_in + ssd_out + ssd_scr, reverse, n_steps, SSD_CHUNKS_PER_STEP,
                          batch_id, step)
    attn_steps = _band_attn_main(attn_in + attn_out + attn_scr, attn_tl, attn_lblocks,
                                 step % attn_lblocks)
    for _ in itertools.chain(ssd_steps, attn_steps):
        pass


def _ssd_with_attn(ssd_plan, ssd_args, reverse, qd, kd, vd, batch, seq_len, dil):
    n_steps = seq_len // SSD_STEP_ROWS
    probe = _attn_plan(qd.shape[0], seq_len, dil, lambda b, i: (b, 0, 0))
    nl = probe["n_lblocks"]
    assert dil * nl == n_steps, "attention blocks must tile the SSD grid one to one"
    attn = _attn_plan(qd.shape[0], seq_len, dil, lambda b, i: (b, i // nl, i % nl))
    outs = pl.pallas_call(
        functools.partial(
            _ssd_with_attn_kernel, reverse=reverse, n_steps=n_steps,
            n_ssd=(len(ssd_plan["in_specs"]), len(ssd_plan["out_specs"]), len(ssd_plan["scratch"])),
            n_attn=(len(attn["in_specs"]), len(attn["out_specs"]), len(attn["scratch"])),
            attn_tl=attn["tl"], attn_lblocks=nl),
        grid=(batch, n_steps),
        in_specs=ssd_plan["in_specs"] + attn["in_specs"],
        out_specs=ssd_plan["out_specs"] + attn["out_specs"],
        out_shape=ssd_plan["out_shape"] + attn["out_shape"],
        scratch_shapes=ssd_plan["scratch"] + attn["scratch"],
        compiler_params=_cparams(("arbitrary", "arbitrary")),
        name=("ssd_bwd" if reverse else "ssd_fwd") + f"_attn_d{dil}",
    )(*ssd_args, qd, kd, kd, kd, vd, vd, vd)
    n_ssd_out = len(ssd_plan["out_specs"])
    return outs[:n_ssd_out], outs[n_ssd_out:]


Q_SUB = 128
K_WIN = Q_SUB + 2 * BAND_RADIUS


def _band_attn_kernel(*refs, tl, n_lblocks):
    first = (pl.program_id(0) == 0) & (pl.program_id(1) == 0) & (pl.program_id(2) == 0)
    _band_attn_body(refs, tl, n_lblocks, first, pl.program_id(2))


def _band_attn_body(refs, tl, n_lblocks, first_step, lb):
    _band_attn_init(refs[-1], first_step)
    for _ in _band_attn_main(refs, tl, n_lblocks, lb):
        pass


def _band_attn_init(bias_scr, first_step):
    r = BAND_RADIUS

    @pl.when(first_step)
    def _():
        t_io = lax.broadcasted_iota(jnp.int32, (2 * Q_SUB, K_WIN), 0) & (Q_SUB - 1)
        u_io = lax.broadcasted_iota(jnp.int32, (2 * Q_SUB, K_WIN), 1)
        band = (u_io >= t_io) & (u_io <= t_io + 2 * r)
        bias_scr[0] = jnp.where(band, 0.0, NEG_BIG)
        bias_scr[1] = jnp.where(band & (u_io >= r), 0.0, NEG_BIG)
        bias_scr[2] = jnp.where(band & (u_io < K_WIN - r), 0.0, NEG_BIG)


def _band_attn_main(refs, tl, n_lblocks, lb):
    (q_ref, k_ref, kp_ref, kn_ref, v_ref, vp_ref, vn_ref,
     o_ref, stat_ref, kbuf, vbuf, bias_scr) = refs
    r = BAND_RADIUS
    kbuf[0:r, :] = kp_ref[...]
    kbuf[r:r + tl, :] = k_ref[...]
    kbuf[r + tl:, :] = kn_ref[...]
    vbuf[0:r, :] = vp_ref[...]
    vbuf[r:r + tl, :] = v_ref[...]
    vbuf[r + tl:, :] = vn_ref[...]

    lane = lax.broadcasted_iota(jnp.int32, (Q_SUB, LANES), 1)
    low_half = lane < HEAD_DIM
    zero_b = jnp.zeros((Q_SUB, LANES), BF16)

    n_sub = tl // Q_SUB
    assert n_sub >= 2, "first and last score tiles of a block must differ"
    for j in range(n_sub):
        if j == 0:
            slot = jnp.where(lb == 0, 1, 0)
        elif j == n_sub - 1:
            slot = jnp.where(lb == n_lblocks - 1, 2, 0)
        else:
            slot = 0
        rows = slice(j * Q_SUB, (j + 1) * Q_SUB)
        stat_ref[rows, :] = jnp.zeros((Q_SUB, LANES), F32)
        for p in range(N_HEADS // 2):
            cols = slice(p * LANES, (p + 1) * LANES)
            qp = q_ref[rows, cols]
            qs = jnp.concatenate([jnp.where(low_half, qp, zero_b),
                                  jnp.where(low_half, zero_b, qp)], axis=0)
            kw = kbuf[j * Q_SUB:j * Q_SUB + K_WIN, cols]
            vw = vbuf[j * Q_SUB:j * Q_SUB + K_WIN, cols]
            s = jnp.einsum('qd,kd->qk', qs, kw, preferred_element_type=F32) + bias_scr[slot]
            m = jnp.max(s, axis=-1, keepdims=True)
            e = jnp.exp2(s - m)
            l = jnp.sum(e, axis=-1, keepdims=True)
            o2 = jnp.dot(e.astype(BF16), vw, preferred_element_type=F32)
            o_ref[rows, cols] = jnp.where(low_half, o2[:Q_SUB], o2[Q_SUB:]).astype(o_ref.dtype)
            for hh, sl in ((2 * p, slice(0, Q_SUB)), (2 * p + 1, slice(Q_SUB, 2 * Q_SUB))):
                stat_ref[rows, hh:hh + 1] = m[sl]
                stat_ref[rows, N_HEADS + hh:N_HEADS + hh + 1] = l[sl]
        yield


def _band_attn(qd, kd, vd, batch, seq_len, dil):
    plan = _attn_plan(qd.shape[0], seq_len, dil, lambda b, c, l: (b, c, l))
    return pl.pallas_call(
        functools.partial(_band_attn_kernel, tl=plan["tl"], n_lblocks=plan["n_lblocks"]),
        grid=(batch, dil, plan["n_lblocks"]),
        in_specs=plan["in_specs"],
        out_specs=plan["out_specs"],
        out_shape=plan["out_shape"],
        scratch_shapes=plan["scratch"],
        compiler_params=_cparams(("arbitrary", "arbitrary", "arbitrary")),
        name=f"band_attn_d{dil}",
    )(qd, kd, kd, kd, vd, vd, vd)


def _attn_plan(n_rows, seq_len, dil, to_bcl):
    sub_len = seq_len // dil
    tl = min(512, sub_len)
    n_lblocks = sub_len // tl
    r = BAND_RADIUS
    halo_per_block = tl // r
    n_halo = n_rows // r

    def main(*ids):
        b, c, l = to_bcl(*ids)
        return (b * n_lblocks + l, c)

    def prev(*ids):
        b, c, l = to_bcl(*ids)
        return (jnp.maximum((b * n_lblocks + l) * halo_per_block - 1, 0), c)

    def nxt(*ids):
        b, c, l = to_bcl(*ids)
        return (jnp.minimum((b * n_lblocks + l + 1) * halo_per_block, n_halo - 1), c)

    blk = pl.BlockSpec((tl, D_ATTN), main)
    hp = pl.BlockSpec((r, D_ATTN), prev)
    hn = pl.BlockSpec((r, D_ATTN), nxt)
    return dict(
        tl=tl, n_lblocks=n_lblocks,
        in_specs=[blk, blk, hp, hn, blk, hp, hn],
        out_specs=[blk, pl.BlockSpec((tl, LANES), main)],
        out_shape=[jax.ShapeDtypeStruct((n_rows, dil * D_ATTN), BF16),
                   jax.ShapeDtypeStruct((n_rows, dil * LANES), F32)],
        scratch=[pltpu.VMEM((tl + 2 * r, D_ATTN), BF16),
                 pltpu.VMEM((tl + 2 * r, D_ATTN), BF16),
                 pltpu.VMEM((3, 2 * Q_SUB, K_WIN), F32)])


def _out_proj_kernel(x_ref, ys_ref, o1_ref, o2_ref, o3_ref, l1_ref, l2_ref, l3_ref,
                     za_ref, wo_ref, gf_ref, y_ref, o_scr, omid_scr, l_scr, lmid_scr):
    tm = x_ref.shape[0]

    def token_order(src_ref, scr, mid, dil, width):
        for j in range(width // LANES):
            if dil == SUB_DIL:
                for c in range(dil):
                    col = c * width + j * LANES
                    scr[j, pl.ds(c, tm // dil, stride=dil), :] = (
                        src_ref[:, col:col + LANES].astype(F32))
            else:
                for c4 in range(SUB_DIL):
                    for c2 in range(SUB_DIL):
                        col = (c4 + SUB_DIL * c2) * width + j * LANES
                        mid[j, c4, pl.ds(c2, tm // dil, stride=SUB_DIL), :] = (
                            src_ref[:, col:col + LANES].astype(F32))
                for c4 in range(SUB_DIL):
                    scr[j, pl.ds(c4, tm // SUB_DIL, stride=SUB_DIL), :] = mid[j, c4]
        return jnp.concatenate([scr[j] for j in range(width // LANES)], axis=1)

    s1 = l1_ref[...]
    s2 = token_order(l2_ref, l_scr, lmid_scr, DILATIONS[1], LANES)
    s3 = token_order(l3_ref, l_scr, lmid_scr, DILATIONS[2], LANES)
    m = jnp.maximum(jnp.maximum(s1, s2), s3)
    head_lane = lax.broadcasted_iota(jnp.int32, s1.shape, 1) < N_HEADS
    a = [jnp.exp2(s - m) for s in (s1, s2, s3)]
    den = sum(ap * pltpu.roll(s, LANES - N_HEADS, 1) for ap, s in zip(a, (s1, s2, s3)))
    inv = jnp.where(head_lane, 1.0 / jnp.where(head_lane, den, 1.0), 0.0)
    lane = lax.broadcasted_iota(jnp.int32, (LANES, D_ATTN), 1)
    rowi = lax.broadcasted_iota(jnp.int32, (LANES, D_ATTN), 0)
    expand = jnp.where(rowi == lax.shift_right_logical(lane, HEAD_SHIFT), 1.0, 0.0).astype(BF16)

    def weight(ap):
        return jnp.dot((ap * inv).astype(BF16), expand, preferred_element_type=F32)

    mix = weight(a[0]) * o1_ref[...].astype(F32)
    mix = mix + weight(a[1]) * token_order(o2_ref, o_scr, omid_scr, DILATIONS[1], D_ATTN)
    mix = mix + weight(a[2]) * token_order(o3_ref, o_scr, omid_scr, DILATIONS[2], D_ATTN)
    z = za_ref[...].astype(F32)
    y_attn = (mix * (z / (1.0 + jnp.exp(-z)))).astype(BF16)
    out = x_ref[...]
    out = out + jnp.dot(ys_ref[...], wo_ref[:D_SSD, :], preferred_element_type=F32)
    out = out + jnp.dot(y_attn, wo_ref[D_SSD:, :], preferred_element_type=F32)
    ms = jnp.mean(out * out, axis=-1, keepdims=True)
    y_ref[...] = out * lax.rsqrt(ms + EPS) * gf_ref[...]


def _out_proj(x2, y_ssd, os_, lses, za, w_out, norm_f, tm):
    t_total = x2.shape[0]
    row = lambda i: (i, 0)
    wide = pl.BlockSpec((tm, D_MODEL), row)
    o_specs = [pl.BlockSpec((tm // d, d * D_ATTN), row) for d in DILATIONS]
    l_specs = [pl.BlockSpec((tm // d, d * LANES), row) for d in DILATIONS]
    return pl.pallas_call(
        _out_proj_kernel,
        grid=(t_total // tm,),
        in_specs=[wide, wide] + o_specs + l_specs
                 + [wide, _const_spec(w_out.shape), _const_spec(norm_f.shape)],
        out_specs=wide,
        out_shape=jax.ShapeDtypeStruct((t_total, D_MODEL), F32),
        scratch_shapes=[pltpu.VMEM((D_ATTN // LANES, tm, LANES), F32),
                        pltpu.VMEM((D_ATTN // LANES, SUB_DIL, tm // SUB_DIL, LANES), F32),
                        pltpu.VMEM((1, tm, LANES), F32),
                        pltpu.VMEM((1, SUB_DIL, tm // SUB_DIL, LANES), F32)],
        compiler_params=_cparams(("arbitrary",)),
        name="out_proj",
    )(x2, y_ssd, *os_, *lses, za, w_out, norm_f)


def _rope_tables(seq_len):
    half = HEAD_DIM // 2
    inv = ROPE_THETA ** (-jnp.arange(half, dtype=F32) * 2.0 / HEAD_DIM)
    ang = jnp.arange(seq_len, dtype=F32)[:, None] * inv[None, :]
    cos, sin = jnp.cos(ang), jnp.sin(ang)
    cos_h = jnp.concatenate([cos, cos], axis=-1)
    sin_h = jnp.concatenate([-sin, sin], axis=-1)
    reps = LANES // HEAD_DIM
    return jnp.tile(cos_h, (1, reps)), jnp.tile(sin_h, (1, reps))


def _pad_lanes(v, offset=0):
    out = jnp.zeros((1, LANES), F32)
    return out.at[0, offset:offset + v.shape[0]].set(v.astype(F32))


def _encoder(x, tables, params):
    (ws, conv_w, conv_b, dtb, alog, dskip_x, norm_ssd, norm_in, w_out, norm_f) = params
    batch, seq_len, _ = x.shape
    t_total = batch * seq_len
    x2 = x.reshape(t_total, D_MODEL)
    cos, sin = tables
    q, k, v, za, zs, xbc, dt = _in_proj(x2, norm_in, cos[:seq_len], sin[:seq_len], ws,
                                        seq_len, tm=IN_PROJ_ROWS)
    fwd_plan = _ssd_fwd_plan(t_total, seq_len,
                             (conv_w.shape, conv_b.shape, dtb.shape, alog.shape))
    (y_f, act), attn4 = _ssd_with_attn(
        fwd_plan, (xbc, xbc, xbc, dt, conv_w, conv_b, dtb, alog), False,
        q[1], k[1], v[1], batch, seq_len, DILATIONS[1])
    bwd_plan = _ssd_bwd_plan(t_total, seq_len,
                             (dtb.shape, alog.shape, dskip_x.shape, norm_ssd.shape))
    (y_ssd,), attn1 = _ssd_with_attn(
        bwd_plan, (act, dt, dtb, alog, y_f, zs, dskip_x, norm_ssd), True,
        q[0], k[0], v[0], batch, seq_len, DILATIONS[0])
    attn16 = _band_attn(q[2], k[2], v[2], batch, seq_len, DILATIONS[2])
    os_, stats = zip(attn1, attn4, attn16)
    y = _out_proj(x2, y_ssd, os_, stats, za, w_out, norm_f, tm=OUT_PROJ_ROWS)
    return y.reshape(batch, seq_len, D_MODEL)


def kernel(x_prompt, x_sample, w_in, conv_w, conv_b, dt_bias_fwd, dt_bias_bwd, a_log_fwd,
           a_log_bwd, d_skip, norm_ssd, norm_in, w_out, norm_f):
    assert w_in.shape[0] == 1, "single-layer encoder"
    w = w_in[0]
    bounds = [0, D_ATTN, 2 * D_ATTN, 3 * D_ATTN, 4 * D_ATTN, 4 * D_ATTN + D_SSD,
              4 * D_ATTN + D_SSD + D_XBC]
    segs = [w[:, bounds[i]:bounds[i + 1]].astype(BF16) for i in range(6)]
    w_dt = jnp.zeros((D_MODEL, LANES), F32).at[:, :2 * N_HEADS].set(w[:, bounds[6]:])
    ws = tuple(segs) + (w_dt.astype(BF16),)
    dtb = _pad_lanes(dt_bias_fwd[0]) + _pad_lanes(dt_bias_bwd[0], N_HEADS)
    alog = _pad_lanes(a_log_fwd[0]) + _pad_lanes(a_log_bwd[0], N_HEADS)
    dskip_x = jnp.repeat(d_skip[0].astype(F32), HEAD_DIM)[None, :]
    params = (ws, conv_w[0].astype(F32), conv_b[0].astype(F32)[None, :], dtb, alog, dskip_x,
              norm_ssd[0].astype(F32)[None, :], norm_in[0].astype(F32)[None, :],
              w_out[0].astype(BF16), norm_f.astype(F32)[None, :])
    tables = _rope_tables(max(x_prompt.shape[1], x_sample.shape[1]))
    return (_encoder(x_prompt, tables, params), _encoder(x_sample, tables, params))
```

```python
import functools
import itertools

import jax
import jax.numpy as jnp
from jax import lax
from jax.experimental import pallas as pl
from jax.experimental.pallas import tpu as pltpu

D_MODEL = 1024
D_ATTN = 1024
D_SSD = 1024
HEAD_DIM = 64
HEAD_SHIFT = 6
N_HEADS = 16
D_STATE = 128
SSD_GROUPS = 2
CONV_WIDTH = 5
CHUNK = 128
ROPE_THETA = 10000.0
EPS = 1e-5
DILATIONS = (1, 4, 16)
SUB_DIL = 4
BAND_RADIUS = 64
D_XBC = D_SSD + 2 * SSD_GROUPS * D_STATE
LANES = 128
HALO_ROWS = 16
NEG_BIG = -1e30
LOG2E = 1.4426950408889634
Q_SCALE = HEAD_DIM ** -0.5 * LOG2E

F32 = jnp.float32
BF16 = jnp.bfloat16

VMEM_LIMIT = 56 * 1024 * 1024
IN_PROJ_ROWS = 512
OUT_PROJ_ROWS = 512
SSD_CHUNKS_PER_STEP = 4


def _cparams(sem):
    return pltpu.CompilerParams(dimension_semantics=sem, vmem_limit_bytes=VMEM_LIMIT)


def _const_spec(shape):
    nd = len(shape)
    return pl.BlockSpec(shape, lambda *_: (0,) * nd)


def _in_proj_kernel(x_ref, g_ref, cos_ref, sin_ref, wq_ref, wk_ref, wv_ref, wza_ref,
                    wzs_ref, wxbc_ref, wdt_ref,
                    q1_ref, q4_ref, q16_ref, k1_ref, k4_ref, k16_ref, v1_ref, v4_ref, v16_ref,
                    za_ref, zs_ref, xbc_ref, dt_ref, blk_scr, sub_scr):
    tm = x_ref.shape[0]

    def emit_dilated(t, j, outs):
        o1_ref, o4_ref, o16_ref = outs
        o1_ref[:, j * LANES:(j + 1) * LANES] = t.astype(BF16)
        blk_scr[...] = t
        for c4 in range(SUB_DIL):
            a = blk_scr[pl.ds(c4, tm // SUB_DIL, stride=SUB_DIL), :]
            col = c4 * D_ATTN + j * LANES
            o4_ref[:, col:col + LANES] = a.astype(BF16)
            sub_scr[c4] = a
        for c4 in range(SUB_DIL):
            for c2 in range(SUB_DIL):
                col = (c4 + SUB_DIL * c2) * D_ATTN + j * LANES
                o16_ref[:, col:col + LANES] = sub_scr[
                    c4, pl.ds(c2, tm // (SUB_DIL * SUB_DIL), stride=SUB_DIL), :].astype(BF16)

    x = x_ref[...]
    ms = jnp.mean(x * x, axis=-1, keepdims=True)
    h = (x * lax.rsqrt(ms + EPS) * g_ref[...]).astype(BF16)

    cos = cos_ref[...]
    sin = sin_ref[...]
    lane = lax.broadcasted_iota(jnp.int32, cos.shape, 1)
    first_half = (lane & (HEAD_DIM - 1)) < (HEAD_DIM // 2)

    def rotary(t):
        partner = jnp.where(first_half,
                            pltpu.roll(t, LANES - HEAD_DIM // 2, 1),
                            pltpu.roll(t, HEAD_DIM // 2, 1))
        return t * cos + partner * sin

    q_outs = (q1_ref, q4_ref, q16_ref)
    k_outs = (k1_ref, k4_ref, k16_ref)
    v_outs = (v1_ref, v4_ref, v16_ref)
    mxu_cols = 2 * LANES
    for jj in range(D_ATTN // mxu_cols):
        sl = slice(jj * mxu_cols, (jj + 1) * mxu_cols)
        tq = jnp.dot(h, wq_ref[:, sl], preferred_element_type=F32)
        tk = jnp.dot(h, wk_ref[:, sl], preferred_element_type=F32)
        tv = jnp.dot(h, wv_ref[:, sl], preferred_element_type=F32)
        for half in range(2):
            hs = slice(half * LANES, (half + 1) * LANES)
            j = 2 * jj + half
            emit_dilated(rotary(tq[:, hs]) * Q_SCALE, j, q_outs)
            emit_dilated(rotary(tk[:, hs]), j, k_outs)
            emit_dilated(tv[:, hs], j, v_outs)

    for w_ref, o_ref in ((wza_ref, za_ref), (wzs_ref, zs_ref),
                         (wxbc_ref, xbc_ref), (wdt_ref, dt_ref)):
        o_ref[...] = jnp.dot(h, w_ref[...],
                             preferred_element_type=F32).astype(o_ref.dtype)


def _in_proj(x2, norm_in, cos, sin, ws, seq_len, tm):
    t_total = x2.shape[0]
    blocks_per_seq = seq_len // tm
    row = lambda i: (i, 0)
    pos = lambda i: (i % blocks_per_seq, 0)
    attn_shapes = [jax.ShapeDtypeStruct((t_total // d, d * D_ATTN), BF16) for d in DILATIONS]
    attn_specs = [pl.BlockSpec((tm // d, d * D_ATTN), row) for d in DILATIONS]
    out_shapes = (attn_shapes * 3
                  + [jax.ShapeDtypeStruct((t_total, D_ATTN), BF16)] * 2
                  + [jax.ShapeDtypeStruct((t_total, D_XBC), BF16),
                     jax.ShapeDtypeStruct((t_total, LANES), F32)])
    weight_specs = [pl.BlockSpec(w.shape, lambda i: (0, 0), pipeline_mode=pl.Buffered(1))
                    for w in ws]
    outs = pl.pallas_call(
        _in_proj_kernel,
        grid=(t_total // tm,),
        in_specs=[pl.BlockSpec((tm, D_MODEL), row), _const_spec((1, D_MODEL)),
                  pl.BlockSpec((tm, LANES), pos), pl.BlockSpec((tm, LANES), pos)]
                 + weight_specs,
        out_specs=attn_specs * 3
                  + [pl.BlockSpec((tm, D_ATTN), row)] * 2
                  + [pl.BlockSpec((tm, D_XBC), row), pl.BlockSpec((tm, LANES), row)],
        out_shape=out_shapes,
        scratch_shapes=[pltpu.VMEM((tm, LANES), F32),
                        pltpu.VMEM((SUB_DIL, tm // SUB_DIL, LANES), F32)],
        compiler_params=_cparams(("arbitrary",)),
        name="in_proj",
    )(x2, norm_in, cos, sin, *ws)
    q, k, v = outs[0:3], outs[3:6], outs[6:9]
    return q, k, v, outs[9], outs[10], outs[11], outs[12]


def _silu(x):
    return x / (1.0 + jnp.exp(-x))


def _softplus(x):
    return jnp.maximum(x, 0.0) + jnp.log(1.0 + jnp.exp(-jnp.abs(x)))


CONV_PAD = (CONV_WIDTH - 1) // 2
CONV_EXT = 2 * CHUNK
CONV_LEAD = (CONV_EXT - CHUNK) // 2
CONV_TAPS = tuple(j for j in range(CONV_WIDTH) if j != CONV_PAD)


def _init_shift_matrix(shift_scr):
    shape = (len(CONV_TAPS) * CHUNK, CONV_EXT)
    row = lax.broadcasted_iota(jnp.int32, shape, 0)
    col = lax.broadcasted_iota(jnp.int32, shape, 1)
    t = row & (CHUNK - 1)
    blk = lax.shift_right_logical(row, CHUNK.bit_length() - 1)
    off = jnp.zeros(shape, jnp.int32)
    for i, j in enumerate(CONV_TAPS):
        off = jnp.where(blk == i, j - CONV_PAD, off)
    shift_scr[...] = jnp.where(col == CONV_LEAD + t + off, 1.0, 0.0).astype(BF16)


def _conv_silu(lo, mid, hi, shift_scr, cw_ref, cb_ref):
    zeros = jnp.zeros((CONV_LEAD - HALO_ROWS, D_XBC), BF16)
    window = jnp.concatenate([zeros, lo, mid, hi, zeros], axis=0)
    shifted = jnp.dot(shift_scr[...], window, preferred_element_type=F32)
    conv = cb_ref[...] + mid.astype(F32) * cw_ref[CONV_PAD:CONV_PAD + 1, :]
    for i, j in enumerate(CONV_TAPS):
        conv = conv + shifted[i * CHUNK:(i + 1) * CHUNK, :] * cw_ref[j:j + 1, :]
    return _silu(conv)


def _ssd_body(refs, reverse, n_steps, cps, batch_id, step):
    if reverse:
        (act_in_ref, dt_ref, dtb_ref, alog_ref, yf_ref, z_ref, dskip_ref, gn_ref,
         y_ref, state_scr) = refs
    else:
        (xbc_ref, prev_ref, next_ref, dt_ref, cw_ref, cb_ref, dtb_ref, alog_ref,
         y_ref, act_out_ref, state_scr, shift_scr) = refs

    @pl.when(step == 0)
    def _():
        state_scr[...] = jnp.zeros_like(state_scr)

    if not reverse:
        @pl.when((batch_id == 0) & (step == 0))
        def _():
            _init_shift_matrix(shift_scr)

        halo_zeros = jnp.zeros((HALO_ROWS, D_XBC), BF16)
        prev = jnp.where(step > 0, prev_ref[...], halo_zeros)
        nxt = jnp.where(step < n_steps - 1, next_ref[...], halo_zeros)

    for ci in range(cps):
        c = (cps - 1 - ci) if reverse else ci
        rows = slice(c * CHUNK, (c + 1) * CHUNK)
        if reverse:
            act = act_in_ref[rows, :].astype(F32)
            extra = (yf_ref[rows, :], z_ref[rows, :], dskip_ref, gn_ref)
        else:
            lo = prev if c == 0 else xbc_ref[c * CHUNK - HALO_ROWS:c * CHUNK, :]
            hi = (nxt if c == cps - 1
                  else xbc_ref[(c + 1) * CHUNK:(c + 1) * CHUNK + HALO_ROWS, :])
            act = _conv_silu(lo, xbc_ref[rows, :], hi, shift_scr, cw_ref, cb_ref)
            act_out_ref[rows, :] = act.astype(act_out_ref.dtype)
            extra = None
        y = _ssd_chunk(act, dt_ref[rows, :], dtb_ref, alog_ref, state_scr, reverse, extra)
        y_ref[rows, :] = y.astype(y_ref.dtype)
        yield


def _ssd_chunk(act, dt_raw, dtb_ref, alog_ref, state_scr, reverse, extra):
    head_off = N_HEADS if reverse else 0
    last = 0 if reverse else CHUNK - 1
    xs = act[:, :D_SSD]
    b_all = act[:, D_SSD:D_SSD + SSD_GROUPS * D_STATE]
    c_all = act[:, D_SSD + SSD_GROUPS * D_STATE:].astype(BF16)

    pre_row = (dt_raw + dtb_ref[...]).T[:2 * N_HEADS]
    dt_row = _softplus(pre_row)
    dt_col = jnp.concatenate(
        [dt_row, jnp.zeros((LANES - 2 * N_HEADS, CHUNK), F32)], axis=0).T
    a_col = dt_col * (-jnp.exp(alog_ref[...]))
    r_io = lax.broadcasted_iota(jnp.int32, (CHUNK, CHUNK), 0)
    c_io = lax.broadcasted_iota(jnp.int32, (CHUNK, CHUNK), 1)
    mask = (c_io >= r_io) if reverse else (c_io <= r_io)
    tri = jnp.where(mask, 1.0, 0.0).astype(BF16)
    acum_col = jnp.zeros((CHUNK, LANES), F32)
    rest = a_col
    for _ in range(3):
        part = rest.astype(BF16)
        acum_col = acum_col + jnp.dot(tri, part, preferred_element_type=F32)
        rest = rest - part.astype(F32)
    acum2_col = acum_col * LOG2E
    acum2_row = acum2_col.T
    dt_row_b = dt_row.astype(BF16)

    lane = lax.broadcasted_iota(jnp.int32, (LANES, D_SSD), 1)
    rowi = lax.broadcasted_iota(jnp.int32, (LANES, D_SSD), 0)
    expand = jnp.where(rowi == head_off + lax.shift_right_logical(lane, HEAD_SHIFT),
                       1.0, 0.0).astype(BF16)
    w_col = dt_col * jnp.exp(acum_col[last:last + 1, :] - acum_col)
    stacked = jnp.concatenate([jnp.exp(acum_col), w_col], axis=0).astype(BF16)
    expanded = jnp.dot(stacked, expand, preferred_element_type=F32)
    ea_x, w_x = expanded[:CHUNK], expanded[CHUNK:]
    xs_b = xs.astype(BF16)
    xw_b = (xs * w_x).astype(BF16)

    state_in = state_scr[...]
    state_b = state_in.astype(BF16)

    lane2 = lax.broadcasted_iota(jnp.int32, (CHUNK, LANES), 1)
    low_half = lane2 < HEAD_DIM
    zero_b = jnp.zeros((CHUNK, LANES), BF16)

    heads_per_group = N_HEADS // SSD_GROUPS
    gw = heads_per_group * HEAD_DIM
    y_parts = []
    s_parts = []
    for g in range(SSD_GROUPS):
        bg_f = b_all[:, g * D_STATE:(g + 1) * D_STATE]
        bg = bg_f.astype(BF16)
        cg = c_all[:, g * D_STATE:(g + 1) * D_STATE]
        cb = jnp.einsum('ln,sn->ls', cg, bg, preferred_element_type=F32).astype(BF16)
        y_off = jnp.dot(cg, state_b[:, g * gw:(g + 1) * gw], preferred_element_type=F32)
        s_parts.append(jnp.dot(bg_f.T.astype(BF16), xw_b[:, g * gw:(g + 1) * gw],
                               preferred_element_type=F32))
        for p in range(heads_per_group // 2):
            h0 = g * heads_per_group + 2 * p
            ms = []
            for hh in (h0, h0 + 1):
                idx = head_off + hh
                seg = acum2_col[:, idx:idx + 1] - acum2_row[idx:idx + 1, :]
                decay = jnp.exp2(jnp.where(mask, seg, NEG_BIG)).astype(BF16)
                ms.append(decay * cb * dt_row_b[idx:idx + 1, :])
            lhs = jnp.concatenate(ms, axis=1)
            xp = xs_b[:, h0 * HEAD_DIM:h0 * HEAD_DIM + LANES]
            rhs = jnp.concatenate([jnp.where(low_half, xp, zero_b),
                                   jnp.where(low_half, zero_b, xp)], axis=0)
            y_diag = jnp.dot(lhs, rhs, preferred_element_type=F32)
            sl = slice(p * LANES, (p + 1) * LANES)
            y_parts.append(y_diag + y_off[:, sl] * ea_x[:, g * gw + p * LANES:g * gw + (p + 1) * LANES])

    state_scr[...] = state_in * ea_x[last:last + 1, :] + jnp.concatenate(s_parts, axis=1)
    y = jnp.concatenate(y_parts, axis=1)

    if reverse:
        yf, z, dskip_ref, gn_ref = extra
        y = y + yf.astype(F32) + dskip_ref[...] * xs
        y = y * _silu(z.astype(F32))
        ms2 = jnp.mean(y * y, axis=-1, keepdims=True)
        y = y * lax.rsqrt(ms2 + EPS) * gn_ref[...]
    return y


SSD_STEP_ROWS = SSD_CHUNKS_PER_STEP * CHUNK


def _ssd_fwd_plan(t_total, seq_len, small_shapes):
    rows = SSD_STEP_ROWS
    n_steps = seq_len // rows
    halo_per_step = rows // HALO_ROWS
    n_halo_blocks = t_total // HALO_ROWS
    sidx = lambda b, i: b * n_steps + i
    main = lambda b, i: (sidx(b, i), 0)
    prev = lambda b, i: (jnp.maximum(sidx(b, i) * halo_per_step - 1, 0), 0)
    nxt = lambda b, i: (jnp.minimum((sidx(b, i) + 1) * halo_per_step, n_halo_blocks - 1), 0)
    return dict(
        in_specs=[pl.BlockSpec((rows, D_XBC), main),
                  pl.BlockSpec((HALO_ROWS, D_XBC), prev),
                  pl.BlockSpec((HALO_ROWS, D_XBC), nxt),
                  pl.BlockSpec((rows, LANES), main)] + [_const_spec(s) for s in small_shapes],
        out_specs=[pl.BlockSpec((rows, D_SSD), main), pl.BlockSpec((rows, D_XBC), main)],
        out_shape=[jax.ShapeDtypeStruct((t_total, D_SSD), BF16),
                   jax.ShapeDtypeStruct((t_total, D_XBC), BF16)],
        scratch=[pltpu.VMEM((D_STATE, D_SSD), F32),
                 pltpu.VMEM((len(CONV_TAPS) * CHUNK, CONV_EXT), BF16)])


def _ssd_bwd_plan(t_total, seq_len, small_shapes):
    rows = SSD_STEP_ROWS
    n_steps = seq_len // rows
    main = lambda b, i: (b * n_steps + n_steps - 1 - i, 0)
    wide = pl.BlockSpec((rows, D_SSD), main)
    dtb_s, alog_s, dskip_s, gn_s = small_shapes
    return dict(
        in_specs=[pl.BlockSpec((rows, D_XBC), main), pl.BlockSpec((rows, LANES), main),
                  _const_spec(dtb_s), _const_spec(alog_s), wide, wide,
                  _const_spec(dskip_s), _const_spec(gn_s)],
        out_specs=[wide],
        out_shape=[jax.ShapeDtypeStruct((t_total, D_SSD), BF16)],
        scratch=[pltpu.VMEM((D_STATE, D_SSD), F32)])


N_ATTN_IN, N_ATTN_OUT, N_ATTN_SCR = 7, 2, 3


def _ssd_with_attn_kernel(*refs, reverse, n_steps, n_ssd, attn_parts):
    it = iter(refs)
    take = lambda n: [next(it) for _ in range(n)]
    ssd_in = take(n_ssd[0])
    attn_in = [take(N_ATTN_IN) for _ in attn_parts]
    ssd_out = take(n_ssd[1])
    attn_out = [take(N_ATTN_OUT) for _ in attn_parts]
    ssd_scr = take(n_ssd[2])
    attn_scr = [take(N_ATTN_SCR) for _ in attn_parts]
    batch_id, step = pl.program_id(0), pl.program_id(1)
    for scr in attn_scr:
        _band_attn_init(scr[-1], (batch_id == 0) & (step == 0))
    bodies = [_ssd_body(ssd_in + ssd_out + ssd_scr, reverse, n_steps, SSD_CHUNKS_PER_STEP,
                        batch_id, step)]
    for (tl, nl, n_res), a_in, a_out, a_scr in zip(attn_parts, attn_in, attn_out, attn_scr):
        bodies.append(_band_attn_main(a_in + a_out + a_scr, tl, nl, step % nl, n_res))
    for _ in itertools.chain(*bodies):
        pass


def _ssd_with_attn(ssd_plan, ssd_args, reverse, attn_inputs, batch, seq_len):
    n_steps = seq_len // SSD_STEP_ROWS
    parts, specs_in, specs_out, shapes_out, scratch, args, names = [], [], [], [], [], [], []
    for qd, kd, vd, dil in attn_inputs:
        nl = _attn_plan(qd.shape[0], seq_len, dil, 1, lambda b, i: (b, 0, 0))["n_lblocks"]
        n_res, rem = divmod(dil * nl, n_steps)
        assert rem == 0 and n_res >= 1 and (n_res == 1 or nl == 1), "attention must tile the grid"
        plan = _attn_plan(qd.shape[0], seq_len, dil, n_res,
                          lambda b, i, nl=nl: (b, i // nl, i % nl))
        parts.append((plan["tl"], nl, n_res))
        specs_in += plan["in_specs"]
        specs_out += plan["out_specs"]
        shapes_out += plan["out_shape"]
        scratch += plan["scratch"]
        args += [qd, kd, kd, kd, vd, vd, vd]
        names.append(f"d{dil}")
    n_ssd_out = len(ssd_plan["out_specs"])
    outs = pl.pallas_call(
        functools.partial(
            _ssd_with_attn_kernel, reverse=reverse, n_steps=n_steps,
            n_ssd=(len(ssd_plan["in_specs"]), n_ssd_out, len(ssd_plan["scratch"])),
            attn_parts=tuple(parts)),
        grid=(batch, n_steps),
        in_specs=ssd_plan["in_specs"] + specs_in,
        out_specs=ssd_plan["out_specs"] + specs_out,
        out_shape=ssd_plan["out_shape"] + shapes_out,
        scratch_shapes=ssd_plan["scratch"] + scratch,
        compiler_params=_cparams(("arbitrary", "arbitrary")),
        name=("ssd_bwd" if reverse else "ssd_fwd") + "_attn_" + "_".join(names),
    )(*ssd_args, *args)
    attn_outs = outs[n_ssd_out:]
    return outs[:n_ssd_out], [tuple(attn_outs[2 * i:2 * i + 2]) for i in range(len(parts))]


Q_SUB = 128
K_WIN = Q_SUB + 2 * BAND_RADIUS


def _band_attn_kernel(*refs, tl, n_lblocks):
    first = (pl.program_id(0) == 0) & (pl.program_id(1) == 0) & (pl.program_id(2) == 0)
    _band_attn_body(refs, tl, n_lblocks, first, pl.program_id(2))


def _band_attn_body(refs, tl, n_lblocks, first_step, lb):
    _band_attn_init(refs[-1], first_step)
    for _ in _band_attn_main(refs, tl, n_lblocks, lb, 1):
        pass


def _band_attn_init(bias_scr, first_step):
    r = BAND_RADIUS

    @pl.when(first_step)
    def _():
        t_io = lax.broadcasted_iota(jnp.int32, (2 * Q_SUB, K_WIN), 0) & (Q_SUB - 1)
        u_io = lax.broadcasted_iota(jnp.int32, (2 * Q_SUB, K_WIN), 1)
        band = (u_io >= t_io) & (u_io <= t_io + 2 * r)
        bias_scr[0] = jnp.where(band, 0.0, NEG_BIG)
        bias_scr[1] = jnp.where(band & (u_io >= r), 0.0, NEG_BIG)
        bias_scr[2] = jnp.where(band & (u_io < K_WIN - r), 0.0, NEG_BIG)


def _band_attn_main(refs, tl, n_lblocks, lb, n_res):
    (q_ref, k_ref, kp_ref, kn_ref, v_ref, vp_ref, vn_ref,
     o_ref, stat_ref, kbuf, vbuf, bias_scr) = refs
    r = BAND_RADIUS
    kbuf[0:r, :] = kp_ref[...]
    kbuf[r:r + tl, :] = k_ref[...]
    kbuf[r + tl:, :] = kn_ref[...]
    vbuf[0:r, :] = vp_ref[...]
    vbuf[r:r + tl, :] = v_ref[...]
    vbuf[r + tl:, :] = vn_ref[...]

    lane = lax.broadcasted_iota(jnp.int32, (Q_SUB, LANES), 1)
    low_half = lane < HEAD_DIM
    zero_b = jnp.zeros((Q_SUB, LANES), BF16)

    n_sub = tl // Q_SUB
    assert n_sub >= 2, "first and last score tiles of a block must differ"
    for res, j in itertools.product(range(n_res), range(n_sub)):
        if j == 0:
            slot = jnp.where(lb == 0, 1, 0)
        elif j == n_sub - 1:
            slot = jnp.where(lb == n_lblocks - 1, 2, 0)
        else:
            slot = 0
        rows = slice(j * Q_SUB, (j + 1) * Q_SUB)
        stat0 = res * LANES
        stat_ref[rows, stat0:stat0 + LANES] = jnp.zeros((Q_SUB, LANES), F32)
        for p in range(N_HEADS // 2):
            cols = slice(res * D_ATTN + p * LANES, res * D_ATTN + (p + 1) * LANES)
            qp = q_ref[rows, cols]
            qs = jnp.concatenate([jnp.where(low_half, qp, zero_b),
                                  jnp.where(low_half, zero_b, qp)], axis=0)
            kw = kbuf[j * Q_SUB:j * Q_SUB + K_WIN, cols]
            vw = vbuf[j * Q_SUB:j * Q_SUB + K_WIN, cols]
            s = jnp.einsum('qd,kd->qk', qs, kw, preferred_element_type=F32) + bias_scr[slot]
            m = jnp.max(s, axis=-1, keepdims=True)
            e = jnp.exp2(s - m)
            l = jnp.sum(e, axis=-1, keepdims=True)
            o2 = jnp.dot(e.astype(BF16), vw, preferred_element_type=F32)
            o_ref[rows, cols] = jnp.where(low_half, o2[:Q_SUB], o2[Q_SUB:]).astype(o_ref.dtype)
            for hh, sl in ((2 * p, slice(0, Q_SUB)), (2 * p + 1, slice(Q_SUB, 2 * Q_SUB))):
                stat_ref[rows, stat0 + hh:stat0 + hh + 1] = m[sl]
                stat_ref[rows, stat0 + N_HEADS + hh:stat0 + N_HEADS + hh + 1] = l[sl]
        yield


def _band_attn(qd, kd, vd, batch, seq_len, dil):
    plan = _attn_plan(qd.shape[0], seq_len, dil, 1, lambda b, c, l: (b, c, l))
    return pl.pallas_call(
        functools.partial(_band_attn_kernel, tl=plan["tl"], n_lblocks=plan["n_lblocks"]),
        grid=(batch, dil, plan["n_lblocks"]),
        in_specs=plan["in_specs"],
        out_specs=plan["out_specs"],
        out_shape=plan["out_shape"],
        scratch_shapes=plan["scratch"],
        compiler_params=_cparams(("arbitrary", "arbitrary", "arbitrary")),
        name=f"band_attn_d{dil}",
    )(qd, kd, kd, kd, vd, vd, vd)


def _attn_plan(n_rows, seq_len, dil, n_res, to_bcl):
    sub_len = seq_len // dil
    tl = min(512, sub_len)
    n_lblocks = sub_len // tl
    r = BAND_RADIUS
    halo_per_block = tl // r
    n_halo = n_rows // r
    width = n_res * D_ATTN

    def main(*ids):
        b, c, l = to_bcl(*ids)
        return (b * n_lblocks + l, c)

    def prev(*ids):
        b, c, l = to_bcl(*ids)
        return (jnp.maximum((b * n_lblocks + l) * halo_per_block - 1, 0), c)

    def nxt(*ids):
        b, c, l = to_bcl(*ids)
        return (jnp.minimum((b * n_lblocks + l + 1) * halo_per_block, n_halo - 1), c)

    blk = pl.BlockSpec((tl, width), main)
    hp = pl.BlockSpec((r, width), prev)
    hn = pl.BlockSpec((r, width), nxt)
    return dict(
        tl=tl, n_lblocks=n_lblocks,
        in_specs=[blk, blk, hp, hn, blk, hp, hn],
        out_specs=[blk, pl.BlockSpec((tl, n_res * LANES), main)],
        out_shape=[jax.ShapeDtypeStruct((n_rows, dil * D_ATTN), BF16),
                   jax.ShapeDtypeStruct((n_rows, dil * LANES), F32)],
        scratch=[pltpu.VMEM((tl + 2 * r, width), BF16),
                 pltpu.VMEM((tl + 2 * r, width), BF16),
                 pltpu.VMEM((3, 2 * Q_SUB, K_WIN), F32)])


def _out_proj_kernel(x_ref, ys_ref, o1_ref, o2_ref, o3_ref, l1_ref, l2_ref, l3_ref,
                     za_ref, wo_ref, gf_ref, y_ref, o_scr, omid_scr, l_scr, lmid_scr):
    tm = x_ref.shape[0]

    def token_order(src_ref, scr, mid, dil, width):
        for j in range(width // LANES):
            if dil == SUB_DIL:
                for c in range(dil):
                    col = c * width + j * LANES
                    scr[j, pl.ds(c, tm // dil, stride=dil), :] = (
                        src_ref[:, col:col + LANES].astype(F32))
            else:
                for c4 in range(SUB_DIL):
                    for c2 in range(SUB_DIL):
                        col = (c4 + SUB_DIL * c2) * width + j * LANES
                        mid[j, c4, pl.ds(c2, tm // dil, stride=SUB_DIL), :] = (
                            src_ref[:, col:col + LANES].astype(F32))
                for c4 in range(SUB_DIL):
                    scr[j, pl.ds(c4, tm // SUB_DIL, stride=SUB_DIL), :] = mid[j, c4]
        return jnp.concatenate([scr[j] for j in range(width // LANES)], axis=1)

    s1 = l1_ref[...]
    s2 = token_order(l2_ref, l_scr, lmid_scr, DILATIONS[1], LANES)
    s3 = token_order(l3_ref, l_scr, lmid_scr, DILATIONS[2], LANES)
    m = jnp.maximum(jnp.maximum(s1, s2), s3)
    head_lane = lax.broadcasted_iota(jnp.int32, s1.shape, 1) < N_HEADS
    a = [jnp.exp2(s - m) for s in (s1, s2, s3)]
    den = sum(ap * pltpu.roll(s, LANES - N_HEADS, 1) for ap, s in zip(a, (s1, s2, s3)))
    inv = jnp.where(head_lane, 1.0 / jnp.where(head_lane, den, 1.0), 0.0)
    lane = lax.broadcasted_iota(jnp.int32, (LANES, D_ATTN), 1)
    rowi = lax.broadcasted_iota(jnp.int32, (LANES, D_ATTN), 0)
    expand = jnp.where(rowi == lax.shift_right_logical(lane, HEAD_SHIFT), 1.0, 0.0).astype(BF16)

    def weight(ap):
        return jnp.dot((ap * inv).astype(BF16), expand, preferred_element_type=F32)

    mix = weight(a[0]) * o1_ref[...].astype(F32)
    mix = mix + weight(a[1]) * token_order(o2_ref, o_scr, omid_scr, DILATIONS[1], D_ATTN)
    mix = mix + weight(a[2]) * token_order(o3_ref, o_scr, omid_scr, DILATIONS[2], D_ATTN)
    z = za_ref[...].astype(F32)
    y_attn = (mix * (z / (1.0 + jnp.exp(-z)))).astype(BF16)
    out = x_ref[...]
    out = out + jnp.dot(ys_ref[...], wo_ref[:D_SSD, :], preferred_element_type=F32)
    out = out + jnp.dot(y_attn, wo_ref[D_SSD:, :], preferred_element_type=F32)
    ms = jnp.mean(out * out, axis=-1, keepdims=True)
    y_ref[...] = out * lax.rsqrt(ms + EPS) * gf_ref[...]


def _out_proj(x2, y_ssd, os_, lses, za, w_out, norm_f, tm):
    t_total = x2.shape[0]
    row = lambda i: (i, 0)
    wide = pl.BlockSpec((tm, D_MODEL), row)
    o_specs = [pl.BlockSpec((tm // d, d * D_ATTN), row) for d in DILATIONS]
    l_specs = [pl.BlockSpec((tm // d, d * LANES), row) for d in DILATIONS]
    return pl.pallas_call(
        _out_proj_kernel,
        grid=(t_total // tm,),
        in_specs=[wide, wide] + o_specs + l_specs
                 + [wide, _const_spec(w_out.shape), _const_spec(norm_f.shape)],
        out_specs=wide,
        out_shape=jax.ShapeDtypeStruct((t_total, D_MODEL), F32),
        scratch_shapes=[pltpu.VMEM((D_ATTN // LANES, tm, LANES), F32),
                        pltpu.VMEM((D_ATTN // LANES, SUB_DIL, tm // SUB_DIL, LANES), F32),
                        pltpu.VMEM((1, tm, LANES), F32),
                        pltpu.VMEM((1, SUB_DIL, tm // SUB_DIL, LANES), F32)],
        compiler_params=_cparams(("arbitrary",)),
        name="out_proj",
    )(x2, y_ssd, *os_, *lses, za, w_out, norm_f)


def _rope_tables(seq_len):
    half = HEAD_DIM // 2
    inv = ROPE_THETA ** (-jnp.arange(half, dtype=F32) * 2.0 / HEAD_DIM)
    ang = jnp.arange(seq_len, dtype=F32)[:, None] * inv[None, :]
    cos, sin = jnp.cos(ang), jnp.sin(ang)
    cos_h = jnp.concatenate([cos, cos], axis=-1)
    sin_h = jnp.concatenate([-sin, sin], axis=-1)
    reps = LANES // HEAD_DIM
    return jnp.tile(cos_h, (1, reps)), jnp.tile(sin_h, (1, reps))


def _pad_lanes(v, offset=0):
    out = jnp.zeros((1, LANES), F32)
    return out.at[0, offset:offset + v.shape[0]].set(v.astype(F32))


def _encoder(x, tables, params):
    (ws, conv_w, conv_b, dtb, alog, dskip_x, norm_ssd, norm_in, w_out, norm_f) = params
    batch, seq_len, _ = x.shape
    t_total = batch * seq_len
    x2 = x.reshape(t_total, D_MODEL)
    cos, sin = tables
    q, k, v, za, zs, xbc, dt = _in_proj(x2, norm_in, cos[:seq_len], sin[:seq_len], ws,
                                        seq_len, tm=IN_PROJ_ROWS)
    fwd_plan = _ssd_fwd_plan(t_total, seq_len,
                             (conv_w.shape, conv_b.shape, dtb.shape, alog.shape))
    (y_f, act), (attn4,) = _ssd_with_attn(
        fwd_plan, (xbc, xbc, xbc, dt, conv_w, conv_b, dtb, alog), False,
        [(q[1], k[1], v[1], DILATIONS[1])], batch, seq_len)
    bwd_plan = _ssd_bwd_plan(t_total, seq_len,
                             (dtb.shape, alog.shape, dskip_x.shape, norm_ssd.shape))
    (y_ssd,), (attn1, attn16) = _ssd_with_attn(
        bwd_plan, (act, dt, dtb, alog, y_f, zs, dskip_x, norm_ssd), True,
        [(q[0], k[0], v[0], DILATIONS[0]), (q[2], k[2], v[2], DILATIONS[2])], batch, seq_len)
    os_, stats = zip(attn1, attn4, attn16)
    y = _out_proj(x2, y_ssd, os_, stats, za, w_out, norm_f, tm=OUT_PROJ_ROWS)
    return y.reshape(batch, seq_len, D_MODEL)


def kernel(x_prompt, x_sample, w_in, conv_w, conv_b, dt_bias_fwd, dt_bias_bwd, a_log_fwd,
           a_log_bwd, d_skip, norm_ssd, norm_in, w_out, norm_f):
    assert w_in.shape[0] == 1, "single-layer encoder"
    w = w_in[0]
    bounds = [0, D_ATTN, 2 * D_ATTN, 3 * D_ATTN, 4 * D_ATTN, 4 * D_ATTN + D_SSD,
              4 * D_ATTN + D_SSD + D_XBC]
    segs = [w[:, bounds[i]:bounds[i + 1]].astype(BF16) for i in range(6)]
    w_dt = jnp.zeros((D_MODEL, LANES), F32).at[:, :2 * N_HEADS].set(w[:, bounds[6]:])
    ws = tuple(segs) + (w_dt.astype(BF16),)
    dtb = _pad_lanes(dt_bias_fwd[0]) + _pad_lanes(dt_bias_bwd[0], N_HEADS)
    alog = _pad_lanes(a_log_fwd[0]) + _pad_lanes(a_log_bwd[0], N_HEADS)
    dskip_x = jnp.repeat(d_skip[0].astype(F32), HEAD_DIM)[None, :]
    params = (ws, conv_w[0].astype(F32), conv_b[0].astype(F32)[None, :], dtb, alog, dskip_x,
              norm_ssd[0].astype(F32)[None, :], norm_in[0].astype(F32)[None, :],
              w_out[0].astype(BF16), norm_f.astype(F32)[None, :])
    tables = _rope_tables(max(x_prompt.shape[1], x_sample.shape[1]))
    return (_encoder(x_prompt, tables, params), _encoder(x_sample, tables, params))
```

```python
import functools
import itertools

import jax
import jax.numpy as jnp
from jax import lax
from jax.experimental import pallas as pl
from jax.experimental.pallas import tpu as pltpu

D_MODEL = 1024
D_ATTN = 1024
D_SSD = 1024
HEAD_DIM = 64
HEAD_SHIFT = 6
N_HEADS = 16
D_STATE = 128
SSD_GROUPS = 2
CONV_WIDTH = 5
CHUNK = 128
ROPE_THETA = 10000.0
EPS = 1e-5
DILATIONS = (1, 4, 16)
SUB_DIL = 4
BAND_RADIUS = 64
D_XBC = D_SSD + 2 * SSD_GROUPS * D_STATE
LANES = 128
HALO_ROWS = 16
NEG_BIG = -1e30
LOG2E = 1.4426950408889634
Q_SCALE = HEAD_DIM ** -0.5 * LOG2E

F32 = jnp.float32
BF16 = jnp.bfloat16

VMEM_LIMIT = 56 * 1024 * 1024
IN_PROJ_ROWS = 512
EMIT_SLOTS = 6
OUT_PROJ_ROWS = 512
SSD_CHUNKS_PER_STEP = 4


def _cparams(sem):
    return pltpu.CompilerParams(dimension_semantics=sem, vmem_limit_bytes=VMEM_LIMIT)


def _const_spec(shape):
    nd = len(shape)
    return pl.BlockSpec(shape, lambda *_: (0,) * nd)


def _in_proj_kernel(x_ref, g_ref, cos_ref, sin_ref, wq_ref, wk_ref, wv_ref, wza_ref,
                    wzs_ref, wxbc_ref, wdt_ref,
                    q1_ref, q4_ref, q16_ref, k1_ref, k4_ref, k16_ref, v1_ref, v4_ref, v16_ref,
                    za_ref, zs_ref, xbc_ref, dt_ref, blk_scr, sub_scr):
    tm = x_ref.shape[0]

    def emit_dilated(t, j, outs):
        o1_ref, o4_ref, o16_ref = outs
        o1_ref[:, j * LANES:(j + 1) * LANES] = t.astype(BF16)
        slot = next(slots)
        blk, sub = blk_scr.at[slot], sub_scr.at[slot]
        blk[...] = t
        for c4 in range(SUB_DIL):
            a = blk[pl.ds(c4, tm // SUB_DIL, stride=SUB_DIL), :]
            col = c4 * D_ATTN + j * LANES
            o4_ref[:, col:col + LANES] = a.astype(BF16)
            sub[c4] = a
        for c4 in range(SUB_DIL):
            for c2 in range(SUB_DIL):
                col = (c4 + SUB_DIL * c2) * D_ATTN + j * LANES
                o16_ref[:, col:col + LANES] = sub[
                    c4, pl.ds(c2, tm // (SUB_DIL * SUB_DIL), stride=SUB_DIL), :].astype(BF16)

    slots = itertools.cycle(range(blk_scr.shape[0]))

    x = x_ref[...]
    ms = jnp.mean(x * x, axis=-1, keepdims=True)
    h = (x * lax.rsqrt(ms + EPS) * g_ref[...]).astype(BF16)

    cos = cos_ref[...]
    sin = sin_ref[...]
    lane = lax.broadcasted_iota(jnp.int32, cos.shape, 1)
    first_half = (lane & (HEAD_DIM - 1)) < (HEAD_DIM // 2)

    def rotary(t):
        partner = jnp.where(first_half,
                            pltpu.roll(t, LANES - HEAD_DIM // 2, 1),
                            pltpu.roll(t, HEAD_DIM // 2, 1))
        return t * cos + partner * sin

    q_outs = (q1_ref, q4_ref, q16_ref)
    k_outs = (k1_ref, k4_ref, k16_ref)
    v_outs = (v1_ref, v4_ref, v16_ref)
    mxu_cols = 2 * LANES
    for jj in range(D_ATTN // mxu_cols):
        sl = slice(jj * mxu_cols, (jj + 1) * mxu_cols)
        tq = jnp.dot(h, wq_ref[:, sl], preferred_element_type=F32)
        tk = jnp.dot(h, wk_ref[:, sl], preferred_element_type=F32)
        tv = jnp.dot(h, wv_ref[:, sl], preferred_element_type=F32)
        for half in range(2):
            hs = slice(half * LANES, (half + 1) * LANES)
            j = 2 * jj + half
            emit_dilated(rotary(tq[:, hs]) * Q_SCALE, j, q_outs)
            emit_dilated(rotary(tk[:, hs]), j, k_outs)
            emit_dilated(tv[:, hs], j, v_outs)

    for w_ref, o_ref in ((wza_ref, za_ref), (wzs_ref, zs_ref),
                         (wxbc_ref, xbc_ref), (wdt_ref, dt_ref)):
        o_ref[...] = jnp.dot(h, w_ref[...],
                             preferred_element_type=F32).astype(o_ref.dtype)


def _in_proj(x2, norm_in, cos, sin, ws, seq_len, tm):
    t_total = x2.shape[0]
    blocks_per_seq = seq_len // tm
    row = lambda i: (i, 0)
    pos = lambda i: (i % blocks_per_seq, 0)
    attn_shapes = [jax.ShapeDtypeStruct((t_total // d, d * D_ATTN), BF16) for d in DILATIONS]
    attn_specs = [pl.BlockSpec((tm // d, d * D_ATTN), row) for d in DILATIONS]
    out_shapes = (attn_shapes * 3
                  + [jax.ShapeDtypeStruct((t_total, D_ATTN), BF16)] * 2
                  + [jax.ShapeDtypeStruct((t_total, D_XBC), BF16),
                     jax.ShapeDtypeStruct((t_total, LANES), F32)])
    weight_specs = [pl.BlockSpec(w.shape, lambda i: (0, 0), pipeline_mode=pl.Buffered(1))
                    for w in ws]
    outs = pl.pallas_call(
        _in_proj_kernel,
        grid=(t_total // tm,),
        in_specs=[pl.BlockSpec((tm, D_MODEL), row), _const_spec((1, D_MODEL)),
                  pl.BlockSpec((tm, LANES), pos), pl.BlockSpec((tm, LANES), pos)]
                 + weight_specs,
        out_specs=attn_specs * 3
                  + [pl.BlockSpec((tm, D_ATTN), row)] * 2
                  + [pl.BlockSpec((tm, D_XBC), row), pl.BlockSpec((tm, LANES), row)],
        out_shape=out_shapes,
        scratch_shapes=[pltpu.VMEM((EMIT_SLOTS, tm, LANES), F32),
                        pltpu.VMEM((EMIT_SLOTS, SUB_DIL, tm // SUB_DIL, LANES), F32)],
        compiler_params=_cparams(("arbitrary",)),
        name="in_proj",
    )(x2, norm_in, cos, sin, *ws)
    q, k, v = outs[0:3], outs[3:6], outs[6:9]
    return q, k, v, outs[9], outs[10], outs[11], outs[12]


def _silu(x):
    h = 0.5 * x
    return h + h * jnp.tanh(h)


def _softplus(x):
    return jnp.maximum(x, 0.0) + jnp.log(1.0 + jnp.exp(-jnp.abs(x)))


CONV_PAD = (CONV_WIDTH - 1) // 2
CONV_EXT = 2 * CHUNK
CONV_LEAD = (CONV_EXT - CHUNK) // 2
CONV_TAPS = tuple(j for j in range(CONV_WIDTH) if j != CONV_PAD)


def _init_shift_matrix(shift_scr):
    shape = (len(CONV_TAPS) * CHUNK, CONV_EXT)
    row = lax.broadcasted_iota(jnp.int32, shape, 0)
    col = lax.broadcasted_iota(jnp.int32, shape, 1)
    t = row & (CHUNK - 1)
    blk = lax.shift_right_logical(row, CHUNK.bit_length() - 1)
    off = jnp.zeros(shape, jnp.int32)
    for i, j in enumerate(CONV_TAPS):
        off = jnp.where(blk == i, j - CONV_PAD, off)
    shift_scr[...] = jnp.where(col == CONV_LEAD + t + off, 1.0, 0.0).astype(BF16)


def _conv_silu(lo, mid, hi, shift_scr, cw_ref, cb_ref):
    zeros = jnp.zeros((CONV_LEAD - HALO_ROWS, D_XBC), BF16)
    window = jnp.concatenate([zeros, lo, mid, hi, zeros], axis=0)
    shifted = jnp.dot(shift_scr[...], window, preferred_element_type=F32)
    conv = cb_ref[...] + mid.astype(F32) * cw_ref[CONV_PAD:CONV_PAD + 1, :]
    for i, j in enumerate(CONV_TAPS):
        conv = conv + shifted[i * CHUNK:(i + 1) * CHUNK, :] * cw_ref[j:j + 1, :]
    return _silu(conv)


def _ssd_body(refs, reverse, n_steps, cps, batch_id, step):
    if reverse:
        (act_in_ref, dt_ref, dtb_ref, alog_ref, yf_ref, z_ref, dskip_ref, gn_ref,
         y_ref, state_scr) = refs
    else:
        (xbc_ref, prev_ref, next_ref, dt_ref, cw_ref, cb_ref, dtb_ref, alog_ref,
         y_ref, act_out_ref, state_scr, shift_scr) = refs

    @pl.when(step == 0)
    def _():
        state_scr[...] = jnp.zeros_like(state_scr)

    if not reverse:
        @pl.when((batch_id == 0) & (step == 0))
        def _():
            _init_shift_matrix(shift_scr)

        halo_zeros = jnp.zeros((HALO_ROWS, D_XBC), BF16)
        prev = jnp.where(step > 0, prev_ref[...], halo_zeros)
        nxt = jnp.where(step < n_steps - 1, next_ref[...], halo_zeros)

    for ci in range(cps):
        c = (cps - 1 - ci) if reverse else ci
        rows = slice(c * CHUNK, (c + 1) * CHUNK)
        if reverse:
            act = act_in_ref[rows, :].astype(F32)
            extra = (yf_ref[rows, :], z_ref[rows, :], dskip_ref, gn_ref)
        else:
            lo = prev if c == 0 else xbc_ref[c * CHUNK - HALO_ROWS:c * CHUNK, :]
            hi = (nxt if c == cps - 1
                  else xbc_ref[(c + 1) * CHUNK:(c + 1) * CHUNK + HALO_ROWS, :])
            act = _conv_silu(lo, xbc_ref[rows, :], hi, shift_scr, cw_ref, cb_ref)
            act_out_ref[rows, :] = act.astype(act_out_ref.dtype)
            extra = None
        y = _ssd_chunk(act, dt_ref[rows, :], dtb_ref, alog_ref, state_scr, reverse, extra)
        y_ref[rows, :] = y.astype(y_ref.dtype)
        yield


def _ssd_chunk(act, dt_raw, dtb_ref, alog_ref, state_scr, reverse, extra):
    head_off = N_HEADS if reverse else 0
    last = 0 if reverse else CHUNK - 1
    xs = act[:, :D_SSD]
    b_all = act[:, D_SSD:D_SSD + SSD_GROUPS * D_STATE]
    c_all = act[:, D_SSD + SSD_GROUPS * D_STATE:].astype(BF16)

    pre_row = (dt_raw + dtb_ref[...]).T[:2 * N_HEADS]
    dt_row = _softplus(pre_row)
    dt_col = jnp.concatenate(
        [dt_row, jnp.zeros((LANES - 2 * N_HEADS, CHUNK), F32)], axis=0).T
    a_col = dt_col * (-jnp.exp(alog_ref[...]))
    r_io = lax.broadcasted_iota(jnp.int32, (CHUNK, CHUNK), 0)
    c_io = lax.broadcasted_iota(jnp.int32, (CHUNK, CHUNK), 1)
    mask = (c_io >= r_io) if reverse else (c_io <= r_io)
    tri = jnp.where(mask, 1.0, 0.0).astype(BF16)
    acum_col = jnp.zeros((CHUNK, LANES), F32)
    rest = a_col
    for _ in range(3):
        part = rest.astype(BF16)
        acum_col = acum_col + jnp.dot(tri, part, preferred_element_type=F32)
        rest = rest - part.astype(F32)
    acum2_col = acum_col * LOG2E
    acum2_row = acum2_col.T
    dt_row_b = dt_row.astype(BF16)

    lane = lax.broadcasted_iota(jnp.int32, (LANES, D_SSD), 1)
    rowi = lax.broadcasted_iota(jnp.int32, (LANES, D_SSD), 0)
    expand = jnp.where(rowi == head_off + lax.shift_right_logical(lane, HEAD_SHIFT),
                       1.0, 0.0).astype(BF16)
    w_col = dt_col * jnp.exp(acum_col[last:last + 1, :] - acum_col)
    stacked = jnp.concatenate([jnp.exp(acum_col), w_col], axis=0).astype(BF16)
    expanded = jnp.dot(stacked, expand, preferred_element_type=F32)
    ea_x, w_x = expanded[:CHUNK], expanded[CHUNK:]
    xs_b = xs.astype(BF16)
    xw_b = (xs * w_x).astype(BF16)

    state_in = state_scr[...]
    state_b = state_in.astype(BF16)

    lane2 = lax.broadcasted_iota(jnp.int32, (CHUNK, LANES), 1)
    low_half = lane2 < HEAD_DIM
    zero_b = jnp.zeros((CHUNK, LANES), BF16)

    heads_per_group = N_HEADS // SSD_GROUPS
    gw = heads_per_group * HEAD_DIM
    y_parts = []
    s_parts = []
    for g in range(SSD_GROUPS):
        bg_f = b_all[:, g * D_STATE:(g + 1) * D_STATE]
        bg = bg_f.astype(BF16)
        cg = c_all[:, g * D_STATE:(g + 1) * D_STATE]
        cb = jnp.einsum('ln,sn->ls', cg, bg, preferred_element_type=F32).astype(BF16)
        y_off = jnp.dot(cg, state_b[:, g * gw:(g + 1) * gw], preferred_element_type=F32)
        s_parts.append(jnp.dot(bg_f.T.astype(BF16), xw_b[:, g * gw:(g + 1) * gw],
                               preferred_element_type=F32))
        for p in range(heads_per_group // 2):
            h0 = g * heads_per_group + 2 * p
            ms = []
            for hh in (h0, h0 + 1):
                idx = head_off + hh
                seg = acum2_col[:, idx:idx + 1] - acum2_row[idx:idx + 1, :]
                decay = jnp.exp2(jnp.where(mask, seg, NEG_BIG)).astype(BF16)
                ms.append(decay * cb * dt_row_b[idx:idx + 1, :])
            lhs = jnp.concatenate(ms, axis=1)
            xp = xs_b[:, h0 * HEAD_DIM:h0 * HEAD_DIM + LANES]
            rhs = jnp.concatenate([jnp.where(low_half, xp, zero_b),
                                   jnp.where(low_half, zero_b, xp)], axis=0)
            y_diag = jnp.dot(lhs, rhs, preferred_element_type=F32)
            sl = slice(p * LANES, (p + 1) * LANES)
            y_parts.append(y_diag + y_off[:, sl] * ea_x[:, g * gw + p * LANES:g * gw + (p + 1) * LANES])

    state_scr[...] = state_in * ea_x[last:last + 1, :] + jnp.concatenate(s_parts, axis=1)
    y = jnp.concatenate(y_parts, axis=1)

    if reverse:
        yf, z, dskip_ref, gn_ref = extra
        y = y + yf.astype(F32) + dskip_ref[...] * xs
        y = y * _silu(z.astype(F32))
        ms2 = jnp.mean(y * y, axis=-1, keepdims=True)
        y = y * lax.rsqrt(ms2 + EPS) * gn_ref[...]
    return y


SSD_STEP_ROWS = SSD_CHUNKS_PER_STEP * CHUNK


def _ssd_fwd_plan(t_total, seq_len, small_shapes):
    rows = SSD_STEP_ROWS
    n_steps = seq_len // rows
    halo_per_step = rows // HALO_ROWS
    n_halo_blocks = t_total // HALO_ROWS
    sidx = lambda b, i: b * n_steps + i
    main = lambda b, i: (sidx(b, i), 0)
    prev = lambda b, i: (jnp.maximum(sidx(b, i) * halo_per_step - 1, 0), 0)
    nxt = lambda b, i: (jnp.minimum((sidx(b, i) + 1) * halo_per_step, n_halo_blocks - 1), 0)
    return dict(
        in_specs=[pl.BlockSpec((rows, D_XBC), main),
                  pl.BlockSpec((HALO_ROWS, D_XBC), prev),
                  pl.BlockSpec((HALO_ROWS, D_XBC), nxt),
                  pl.BlockSpec((rows, LANES), main)] + [_const_spec(s) for s in small_shapes],
        out_specs=[pl.BlockSpec((rows, D_SSD), main), pl.BlockSpec((rows, D_XBC), main)],
        out_shape=[jax.ShapeDtypeStruct((t_total, D_SSD), BF16),
                   jax.ShapeDtypeStruct((t_total, D_XBC), BF16)],
        scratch=[pltpu.VMEM((D_STATE, D_SSD), F32),
                 pltpu.VMEM((len(CONV_TAPS) * CHUNK, CONV_EXT), BF16)])


def _ssd_bwd_plan(t_total, seq_len, small_shapes):
    rows = SSD_STEP_ROWS
    n_steps = seq_len // rows
    main = lambda b, i: (b * n_steps + n_steps - 1 - i, 0)
    wide = pl.BlockSpec((rows, D_SSD), main)
    dtb_s, alog_s, dskip_s, gn_s = small_shapes
    return dict(
        in_specs=[pl.BlockSpec((rows, D_XBC), main), pl.BlockSpec((rows, LANES), main),
                  _const_spec(dtb_s), _const_spec(alog_s), wide, wide,
                  _const_spec(dskip_s), _const_spec(gn_s)],
        out_specs=[wide],
        out_shape=[jax.ShapeDtypeStruct((t_total, D_SSD), BF16)],
        scratch=[pltpu.VMEM((D_STATE, D_SSD), F32)])


N_ATTN_IN, N_ATTN_OUT, N_ATTN_SCR = 7, 2, 1


def _ssd_with_attn_kernel(*refs, reverse, n_steps, n_ssd, attn_parts):
    it = iter(refs)
    take = lambda n: [next(it) for _ in range(n)]
    ssd_in = take(n_ssd[0])
    attn_in = [take(N_ATTN_IN) for _ in attn_parts]
    ssd_out = take(n_ssd[1])
    attn_out = [take(N_ATTN_OUT) for _ in attn_parts]
    ssd_scr = take(n_ssd[2])
    attn_scr = [take(N_ATTN_SCR) for _ in attn_parts]
    batch_id, step = pl.program_id(0), pl.program_id(1)
    for scr in attn_scr:
        _band_attn_init(scr[-1], (batch_id == 0) & (step == 0))
    bodies = [_ssd_body(ssd_in + ssd_out + ssd_scr, reverse, n_steps, SSD_CHUNKS_PER_STEP,
                        batch_id, step)]
    for (tl, nl, n_res), a_in, a_out, a_scr in zip(attn_parts, attn_in, attn_out, attn_scr):
        bodies.append(_band_attn_main(a_in + a_out + a_scr, tl, nl, step % nl, n_res))
    for _ in itertools.chain(*bodies):
        pass


def _ssd_with_attn(ssd_plan, ssd_args, reverse, attn_inputs, batch, seq_len):
    n_steps = seq_len // SSD_STEP_ROWS
    parts, specs_in, specs_out, shapes_out, scratch, args, names = [], [], [], [], [], [], []
    for qd, kd, vd, dil in attn_inputs:
        nl = _attn_plan(qd.shape[0], seq_len, dil, 1, lambda b, i: (b, 0, 0))["n_lblocks"]
        n_res, rem = divmod(dil * nl, n_steps)
        assert rem == 0 and n_res >= 1 and (n_res == 1 or nl == 1), "attention must tile the grid"
        plan = _attn_plan(qd.shape[0], seq_len, dil, n_res,
                          lambda b, i, nl=nl: (b, i // nl, i % nl))
        parts.append((plan["tl"], nl, n_res))
        specs_in += plan["in_specs"]
        specs_out += plan["out_specs"]
        shapes_out += plan["out_shape"]
        scratch += plan["scratch"]
        args += [qd, kd, kd, kd, vd, vd, vd]
        names.append(f"d{dil}")
    n_ssd_out = len(ssd_plan["out_specs"])
    outs = pl.pallas_call(
        functools.partial(
            _ssd_with_attn_kernel, reverse=reverse, n_steps=n_steps,
            n_ssd=(len(ssd_plan["in_specs"]), n_ssd_out, len(ssd_plan["scratch"])),
            attn_parts=tuple(parts)),
        grid=(batch, n_steps),
        in_specs=ssd_plan["in_specs"] + specs_in,
        out_specs=ssd_plan["out_specs"] + specs_out,
        out_shape=ssd_plan["out_shape"] + shapes_out,
        scratch_shapes=ssd_plan["scratch"] + scratch,
        compiler_params=_cparams(("arbitrary", "arbitrary")),
        name=("ssd_bwd" if reverse else "ssd_fwd") + "_attn_" + "_".join(names),
    )(*ssd_args, *args)
    attn_outs = outs[n_ssd_out:]
    return outs[:n_ssd_out], [tuple(attn_outs[2 * i:2 * i + 2]) for i in range(len(parts))]


Q_SUB = 128
K_WIN = Q_SUB + 2 * BAND_RADIUS


def _band_attn_kernel(*refs, tl, n_lblocks):
    first = (pl.program_id(0) == 0) & (pl.program_id(1) == 0) & (pl.program_id(2) == 0)
    _band_attn_body(refs, tl, n_lblocks, first, pl.program_id(2))


def _band_attn_body(refs, tl, n_lblocks, first_step, lb):
    _band_attn_init(refs[-1], first_step)
    for _ in _band_attn_main(refs, tl, n_lblocks, lb, 1):
        pass


def _band_attn_init(bias_scr, first_step):
    r = BAND_RADIUS

    @pl.when(first_step)
    def _():
        t_io = lax.broadcasted_iota(jnp.int32, (2 * Q_SUB, K_WIN), 0) & (Q_SUB - 1)
        u_io = lax.broadcasted_iota(jnp.int32, (2 * Q_SUB, K_WIN), 1)
        band = (u_io >= t_io) & (u_io <= t_io + 2 * r)
        bias_scr[0] = jnp.where(band, 0.0, NEG_BIG)
        bias_scr[1] = jnp.where(band & (u_io >= r), 0.0, NEG_BIG)
        bias_scr[2] = jnp.where(band & (u_io < K_WIN - r), 0.0, NEG_BIG)


def _band_attn_main(refs, tl, n_lblocks, lb, n_res):
    (q_ref, k_ref, kp_ref, kn_ref, v_ref, vp_ref, vn_ref,
     o_ref, stat_ref, bias_scr) = refs
    r = BAND_RADIUS

    def window(main_ref, prev_ref, next_ref, j, cols):
        lo = j * Q_SUB - r
        hi = lo + K_WIN
        if lo < 0:
            return jnp.concatenate([prev_ref[:, cols], main_ref[0:hi, cols]], axis=0)
        if hi > tl:
            return jnp.concatenate([main_ref[lo:tl, cols], next_ref[:, cols]], axis=0)
        return main_ref[lo:hi, cols]

    lane = lax.broadcasted_iota(jnp.int32, (Q_SUB, LANES), 1)
    low_half = lane < HEAD_DIM
    zero_b = jnp.zeros((Q_SUB, LANES), BF16)

    n_sub = tl // Q_SUB
    assert n_sub >= 2, "first and last score tiles of a block must differ"
    for res, j in itertools.product(range(n_res), range(n_sub)):
        if j == 0:
            slot = jnp.where(lb == 0, 1, 0)
        elif j == n_sub - 1:
            slot = jnp.where(lb == n_lblocks - 1, 2, 0)
        else:
            slot = 0
        rows = slice(j * Q_SUB, (j + 1) * Q_SUB)
        stat0 = res * LANES
        stat_ref[rows, stat0:stat0 + LANES] = jnp.zeros((Q_SUB, LANES), F32)
        for p in range(N_HEADS // 2):
            cols = slice(res * D_ATTN + p * LANES, res * D_ATTN + (p + 1) * LANES)
            qp = q_ref[rows, cols]
            qs = jnp.concatenate([jnp.where(low_half, qp, zero_b),
                                  jnp.where(low_half, zero_b, qp)], axis=0)
            kw = window(k_ref, kp_ref, kn_ref, j, cols)
            vw = window(v_ref, vp_ref, vn_ref, j, cols)
            s = jnp.einsum('qd,kd->qk', qs, kw, preferred_element_type=F32) + bias_scr[slot]
            m = jnp.max(s, axis=-1, keepdims=True)
            e = jnp.exp2(s - m)
            l = jnp.sum(e, axis=-1, keepdims=True)
            o2 = jnp.dot(e.astype(BF16), vw, preferred_element_type=F32)
            o_ref[rows, cols] = jnp.where(low_half, o2[:Q_SUB], o2[Q_SUB:]).astype(o_ref.dtype)
            for hh, sl in ((2 * p, slice(0, Q_SUB)), (2 * p + 1, slice(Q_SUB, 2 * Q_SUB))):
                stat_ref[rows, stat0 + hh:stat0 + hh + 1] = m[sl]
                stat_ref[rows, stat0 + N_HEADS + hh:stat0 + N_HEADS + hh + 1] = l[sl]
        yield


def _band_attn(qd, kd, vd, batch, seq_len, dil):
    plan = _attn_plan(qd.shape[0], seq_len, dil, 1, lambda b, c, l: (b, c, l))
    return pl.pallas_call(
        functools.partial(_band_attn_kernel, tl=plan["tl"], n_lblocks=plan["n_lblocks"]),
        grid=(batch, dil, plan["n_lblocks"]),
        in_specs=plan["in_specs"],
        out_specs=plan["out_specs"],
        out_shape=plan["out_shape"],
        scratch_shapes=plan["scratch"],
        compiler_params=_cparams(("arbitrary", "arbitrary", "arbitrary")),
        name=f"band_attn_d{dil}",
    )(qd, kd, kd, kd, vd, vd, vd)


def _attn_plan(n_rows, seq_len, dil, n_res, to_bcl):
    sub_len = seq_len // dil
    tl = min(512, sub_len)
    n_lblocks = sub_len // tl
    r = BAND_RADIUS
    halo_per_block = tl // r
    n_halo = n_rows // r
    width = n_res * D_ATTN

    def main(*ids):
        b, c, l = to_bcl(*ids)
        return (b * n_lblocks + l, c)

    def prev(*ids):
        b, c, l = to_bcl(*ids)
        return (jnp.maximum((b * n_lblocks + l) * halo_per_block - 1, 0), c)

    def nxt(*ids):
        b, c, l = to_bcl(*ids)
        return (jnp.minimum((b * n_lblocks + l + 1) * halo_per_block, n_halo - 1), c)

    blk = pl.BlockSpec((tl, width), main)
    hp = pl.BlockSpec((r, width), prev)
    hn = pl.BlockSpec((r, width), nxt)
    return dict(
        tl=tl, n_lblocks=n_lblocks,
        in_specs=[blk, blk, hp, hn, blk, hp, hn],
        out_specs=[blk, pl.BlockSpec((tl, n_res * LANES), main)],
        out_shape=[jax.ShapeDtypeStruct((n_rows, dil * D_ATTN), BF16),
                   jax.ShapeDtypeStruct((n_rows, dil * LANES), F32)],
        scratch=[pltpu.VMEM((3, 2 * Q_SUB, K_WIN), F32)])


def _out_proj_kernel(x_ref, ys_ref, o1_ref, o2_ref, o3_ref, l1_ref, l2_ref, l3_ref,
                     za_ref, wo_ref, gf_ref, y_ref, o_scr, omid_scr, l_scr, lmid_scr):
    tm = x_ref.shape[0]

    def token_order(src_ref, scr, mid, dil, width):
        for j in range(width // LANES):
            if dil == SUB_DIL:
                for c in range(dil):
                    col = c * width + j * LANES
                    scr[j, pl.ds(c, tm // dil, stride=dil), :] = (
                        src_ref[:, col:col + LANES].astype(F32))
            else:
                for c4 in range(SUB_DIL):
                    for c2 in range(SUB_DIL):
                        col = (c4 + SUB_DIL * c2) * width + j * LANES
                        mid[j, c4, pl.ds(c2, tm // dil, stride=SUB_DIL), :] = (
                            src_ref[:, col:col + LANES].astype(F32))
                for c4 in range(SUB_DIL):
                    scr[j, pl.ds(c4, tm // SUB_DIL, stride=SUB_DIL), :] = mid[j, c4]
        return jnp.concatenate([scr[j] for j in range(width // LANES)], axis=1)

    s1 = l1_ref[...]
    s2 = token_order(l2_ref, l_scr.at[0], lmid_scr, DILATIONS[1], LANES)
    s3 = token_order(l3_ref, l_scr.at[1], lmid_scr, DILATIONS[2], LANES)
    m = jnp.maximum(jnp.maximum(s1, s2), s3)
    head_lane = lax.broadcasted_iota(jnp.int32, s1.shape, 1) < N_HEADS
    a = [jnp.exp2(s - m) for s in (s1, s2, s3)]
    den = sum(ap * pltpu.roll(s, LANES - N_HEADS, 1) for ap, s in zip(a, (s1, s2, s3)))
    inv = jnp.where(head_lane, 1.0 / jnp.where(head_lane, den, 1.0), 0.0)
    lane = lax.broadcasted_iota(jnp.int32, (LANES, D_ATTN), 1)
    rowi = lax.broadcasted_iota(jnp.int32, (LANES, D_ATTN), 0)
    expand = jnp.where(rowi == lax.shift_right_logical(lane, HEAD_SHIFT), 1.0, 0.0).astype(BF16)

    def weight(ap):
        return jnp.dot((ap * inv).astype(BF16), expand, preferred_element_type=F32)

    mix = weight(a[0]) * o1_ref[...].astype(F32)
    mix = mix + weight(a[1]) * token_order(o2_ref, o_scr.at[0], omid_scr, DILATIONS[1], D_ATTN)
    mix = mix + weight(a[2]) * token_order(o3_ref, o_scr.at[1], omid_scr, DILATIONS[2], D_ATTN)
    y_attn = (mix * _silu(za_ref[...].astype(F32))).astype(BF16)
    out = x_ref[...]
    out = out + jnp.dot(ys_ref[...], wo_ref[:D_SSD, :], preferred_element_type=F32)
    out = out + jnp.dot(y_attn, wo_ref[D_SSD:, :], preferred_element_type=F32)
    ms = jnp.mean(out * out, axis=-1, keepdims=True)
    y_ref[...] = out * lax.rsqrt(ms + EPS) * gf_ref[...]


def _out_proj(x2, y_ssd, os_, lses, za, w_out, norm_f, tm):
    t_total = x2.shape[0]
    row = lambda i: (i, 0)
    wide = pl.BlockSpec((tm, D_MODEL), row)
    o_specs = [pl.BlockSpec((tm // d, d * D_ATTN), row) for d in DILATIONS]
    l_specs = [pl.BlockSpec((tm // d, d * LANES), row) for d in DILATIONS]
    return pl.pallas_call(
        _out_proj_kernel,
        grid=(t_total // tm,),
        in_specs=[wide, wide] + o_specs + l_specs
                 + [wide, _const_spec(w_out.shape), _const_spec(norm_f.shape)],
        out_specs=wide,
        out_shape=jax.ShapeDtypeStruct((t_total, D_MODEL), F32),
        scratch_shapes=[pltpu.VMEM((2, D_ATTN // LANES, tm, LANES), F32),
                        pltpu.VMEM((D_ATTN // LANES, SUB_DIL, tm // SUB_DIL, LANES), F32),
                        pltpu.VMEM((2, 1, tm, LANES), F32),
                        pltpu.VMEM((1, SUB_DIL, tm // SUB_DIL, LANES), F32)],
        compiler_params=_cparams(("arbitrary",)),
        name="out_proj",
    )(x2, y_ssd, *os_, *lses, za, w_out, norm_f)


def _rope_tables(seq_len):
    half = HEAD_DIM // 2
    inv = ROPE_THETA ** (-jnp.arange(half, dtype=F32) * 2.0 / HEAD_DIM)
    ang = jnp.arange(seq_len, dtype=F32)[:, None] * inv[None, :]
    cos, sin = jnp.cos(ang), jnp.sin(ang)
    cos_h = jnp.concatenate([cos, cos], axis=-1)
    sin_h = jnp.concatenate([-sin, sin], axis=-1)
    reps = LANES // HEAD_DIM
    return jnp.tile(cos_h, (1, reps)), jnp.tile(sin_h, (1, reps))


def _pad_lanes(v, offset=0):
    out = jnp.zeros((1, LANES), F32)
    return out.at[0, offset:offset + v.shape[0]].set(v.astype(F32))


def _encoder(x, tables, params):
    (ws, conv_w, conv_b, dtb, alog, dskip_x, norm_ssd, norm_in, w_out, norm_f) = params
    batch, seq_len, _ = x.shape
    t_total = batch * seq_len
    x2 = x.reshape(t_total, D_MODEL)
    cos, sin = tables
    q, k, v, za, zs, xbc, dt = _in_proj(x2, norm_in, cos[:seq_len], sin[:seq_len], ws,
                                        seq_len, tm=IN_PROJ_ROWS)
    fwd_plan = _ssd_fwd_plan(t_total, seq_len,
                             (conv_w.shape, conv_b.shape, dtb.shape, alog.shape))
    (y_f, act), (attn4,) = _ssd_with_attn(
        fwd_plan, (xbc, xbc, xbc, dt, conv_w, conv_b, dtb, alog), False,
        [(q[1], k[1], v[1], DILATIONS[1])], batch, seq_len)
    bwd_plan = _ssd_bwd_plan(t_total, seq_len,
                             (dtb.shape, alog.shape, dskip_x.shape, norm_ssd.shape))
    (y_ssd,), (attn1, attn16) = _ssd_with_attn(
        bwd_plan, (act, dt, dtb, alog, y_f, zs, dskip_x, norm_ssd), True,
        [(q[0], k[0], v[0], DILATIONS[0]), (q[2], k[2], v[2], DILATIONS[2])], batch, seq_len)
    os_, stats = zip(attn1, attn4, attn16)
    y = _out_proj(x2, y_ssd, os_, stats, za, w_out, norm_f, tm=OUT_PROJ_ROWS)
    return y.reshape(batch, seq_len, D_MODEL)


def kernel(x_prompt, x_sample, w_in, conv_w, conv_b, dt_bias_fwd, dt_bias_bwd, a_log_fwd,
           a_log_bwd, d_skip, norm_ssd, norm_in, w_out, norm_f):
    assert w_in.shape[0] == 1, "single-layer encoder"
    w = w_in[0]
    bounds = [0, D_ATTN, 2 * D_ATTN, 3 * D_ATTN, 4 * D_ATTN, 4 * D_ATTN + D_SSD,
              4 * D_ATTN + D_SSD + D_XBC]
    segs = [w[:, bounds[i]:bounds[i + 1]].astype(BF16) for i in range(6)]
    w_dt = jnp.zeros((D_MODEL, LANES), F32).at[:, :2 * N_HEADS].set(w[:, bounds[6]:])
    ws = tuple(segs) + (w_dt.astype(BF16),)
    dtb = _pad_lanes(dt_bias_fwd[0]) + _pad_lanes(dt_bias_bwd[0], N_HEADS)
    alog = _pad_lanes(a_log_fwd[0]) + _pad_lanes(a_log_bwd[0], N_HEADS)
    dskip_x = jnp.repeat(d_skip[0].astype(F32), HEAD_DIM)[None, :]
    params = (ws, conv_w[0].astype(F32), conv_b[0].astype(F32)[None, :], dtb, alog, dskip_x,
              norm_ssd[0].astype(F32)[None, :], norm_in[0].astype(F32)[None, :],
              w_out[0].astype(BF16), norm_f.astype(F32)[None, :])
    tables = _rope_tables(max(x_prompt.shape[1], x_sample.shape[1]))
    return (_encoder(x_prompt, tables, params), _encoder(x_sample, tables, params))
```

```python
import functools
import itertools

import jax
import jax.numpy as jnp
from jax import lax
from jax.experimental import pallas as pl
from jax.experimental.pallas import tpu as pltpu

D_MODEL = 1024
D_ATTN = 1024
D_SSD = 1024
HEAD_DIM = 64
HEAD_SHIFT = 6
N_HEADS = 16
D_STATE = 128
SSD_GROUPS = 2
CONV_WIDTH = 5
CHUNK = 128
ROPE_THETA = 10000.0
EPS = 1e-5
DILATIONS = (1, 4, 16)
SUB_DIL = 4
BAND_RADIUS = 64
D_XBC = D_SSD + 2 * SSD_GROUPS * D_STATE
LANES = 128
HALO_ROWS = 16
NEG_BIG = -1e30
LOG2E = 1.4426950408889634
Q_SCALE = HEAD_DIM ** -0.5 * LOG2E

F32 = jnp.float32
BF16 = jnp.bfloat16

VMEM_LIMIT = 56 * 1024 * 1024
IN_PROJ_ROWS = 512
EMIT_SLOTS = 6
OUT_PROJ_ROWS = 512
SSD_CHUNKS_PER_STEP = 4


def _cparams(sem):
    return pltpu.CompilerParams(dimension_semantics=sem, vmem_limit_bytes=VMEM_LIMIT)


def _const_spec(shape):
    nd = len(shape)
    return pl.BlockSpec(shape, lambda *_: (0,) * nd)


def _in_proj_kernel(x_ref, g_ref, cos_ref, sin_ref, wq_ref, wk_ref, wv_ref, wza_ref,
                    wzs_ref, wxbc_ref, wdt_ref,
                    q1_ref, q4_ref, q16_ref, k1_ref, k4_ref, k16_ref, v1_ref, v4_ref, v16_ref,
                    za_ref, zs_ref, xbc_ref, dt_ref, blk_scr, sub_scr):
    tm = x_ref.shape[0]

    def emit_dilated(t, j, outs):
        o1_ref, o4_ref, o16_ref = outs
        o1_ref[:, j * LANES:(j + 1) * LANES] = t.astype(BF16)
        slot = next(slots)
        blk, sub = blk_scr.at[slot], sub_scr.at[slot]
        blk[...] = t
        for c4 in range(SUB_DIL):
            a = blk[pl.ds(c4, tm // SUB_DIL, stride=SUB_DIL), :]
            col = c4 * D_ATTN + j * LANES
            o4_ref[:, col:col + LANES] = a.astype(BF16)
            sub[c4] = a
        for c4 in range(SUB_DIL):
            for c2 in range(SUB_DIL):
                col = (c4 + SUB_DIL * c2) * D_ATTN + j * LANES
                o16_ref[:, col:col + LANES] = sub[
                    c4, pl.ds(c2, tm // (SUB_DIL * SUB_DIL), stride=SUB_DIL), :].astype(BF16)

    slots = itertools.cycle(range(blk_scr.shape[0]))

    x = x_ref[...]
    ms = jnp.mean(x * x, axis=-1, keepdims=True)
    h = (x * lax.rsqrt(ms + EPS) * g_ref[...]).astype(BF16)

    cos = cos_ref[...]
    sin = sin_ref[...]
    lane = lax.broadcasted_iota(jnp.int32, cos.shape, 1)
    first_half = (lane & (HEAD_DIM - 1)) < (HEAD_DIM // 2)

    def rotary(t):
        partner = jnp.where(first_half,
                            pltpu.roll(t, LANES - HEAD_DIM // 2, 1),
                            pltpu.roll(t, HEAD_DIM // 2, 1))
        return t * cos + partner * sin

    q_outs = (q1_ref, q4_ref, q16_ref)
    k_outs = (k1_ref, k4_ref, k16_ref)
    v_outs = (v1_ref, v4_ref, v16_ref)
    mxu_cols = 2 * LANES
    for jj in range(D_ATTN // mxu_cols):
        sl = slice(jj * mxu_cols, (jj + 1) * mxu_cols)
        tq = jnp.dot(h, wq_ref[:, sl], preferred_element_type=F32)
        tk = jnp.dot(h, wk_ref[:, sl], preferred_element_type=F32)
        tv = jnp.dot(h, wv_ref[:, sl], preferred_element_type=F32)
        for half in range(2):
            hs = slice(half * LANES, (half + 1) * LANES)
            j = 2 * jj + half
            emit_dilated(rotary(tq[:, hs]) * Q_SCALE, j, q_outs)
            emit_dilated(rotary(tk[:, hs]), j, k_outs)
            emit_dilated(tv[:, hs], j, v_outs)

    for w_ref, o_ref in ((wza_ref, za_ref), (wzs_ref, zs_ref),
                         (wxbc_ref, xbc_ref), (wdt_ref, dt_ref)):
        o_ref[...] = jnp.dot(h, w_ref[...],
                             preferred_element_type=F32).astype(o_ref.dtype)


def _in_proj(x2, norm_in, cos, sin, ws, seq_len, tm):
    t_total = x2.shape[0]
    blocks_per_seq = seq_len // tm
    row = lambda i: (i, 0)
    pos = lambda i: (i % blocks_per_seq, 0)
    attn_shapes = [jax.ShapeDtypeStruct((t_total // d, d * D_ATTN), BF16) for d in DILATIONS]
    attn_specs = [pl.BlockSpec((tm // d, d * D_ATTN), row) for d in DILATIONS]
    out_shapes = (attn_shapes * 3
                  + [jax.ShapeDtypeStruct((t_total, D_ATTN), BF16)] * 2
                  + [jax.ShapeDtypeStruct((t_total, D_XBC), BF16),
                     jax.ShapeDtypeStruct((t_total, LANES), F32)])
    weight_specs = [pl.BlockSpec(w.shape, lambda i: (0, 0), pipeline_mode=pl.Buffered(1))
                    for w in ws]
    outs = pl.pallas_call(
        _in_proj_kernel,
        grid=(t_total // tm,),
        in_specs=[pl.BlockSpec((tm, D_MODEL), row), _const_spec((1, D_MODEL)),
                  pl.BlockSpec((tm, LANES), pos), pl.BlockSpec((tm, LANES), pos)]
                 + weight_specs,
        out_specs=attn_specs * 3
                  + [pl.BlockSpec((tm, D_ATTN), row)] * 2
                  + [pl.BlockSpec((tm, D_XBC), row), pl.BlockSpec((tm, LANES), row)],
        out_shape=out_shapes,
        scratch_shapes=[pltpu.VMEM((EMIT_SLOTS, tm, LANES), F32),
                        pltpu.VMEM((EMIT_SLOTS, SUB_DIL, tm // SUB_DIL, LANES), F32)],
        compiler_params=_cparams(("arbitrary",)),
        name="in_proj",
    )(x2, norm_in, cos, sin, *ws)
    q, k, v = outs[0:3], outs[3:6], outs[6:9]
    return q, k, v, outs[9], outs[10], outs[11], outs[12]


def _silu(x):
    h = 0.5 * x
    return h + h * jnp.tanh(h)


def _softplus(x):
    return jnp.maximum(x, 0.0) + jnp.log(1.0 + jnp.exp(-jnp.abs(x)))


CONV_PAD = (CONV_WIDTH - 1) // 2
CONV_EXT = 2 * CHUNK
CONV_LEAD = (CONV_EXT - CHUNK) // 2
CONV_TAPS = tuple(j for j in range(CONV_WIDTH) if j != CONV_PAD)


def _init_shift_matrix(shift_scr):
    shape = (len(CONV_TAPS) * CHUNK, CONV_EXT)
    row = lax.broadcasted_iota(jnp.int32, shape, 0)
    col = lax.broadcasted_iota(jnp.int32, shape, 1)
    t = row & (CHUNK - 1)
    blk = lax.shift_right_logical(row, CHUNK.bit_length() - 1)
    off = jnp.zeros(shape, jnp.int32)
    for i, j in enumerate(CONV_TAPS):
        off = jnp.where(blk == i, j - CONV_PAD, off)
    shift_scr[...] = jnp.where(col == CONV_LEAD + t + off, 1.0, 0.0).astype(BF16)


def _conv_silu(lo, mid, hi, shift_scr, cw_ref, cb_ref):
    zeros = jnp.zeros((CONV_LEAD - HALO_ROWS, D_XBC), BF16)
    window = jnp.concatenate([zeros, lo, mid, hi, zeros], axis=0)
    shifted = jnp.dot(shift_scr[...], window, preferred_element_type=F32)
    conv = cb_ref[...] + mid.astype(F32) * cw_ref[CONV_PAD:CONV_PAD + 1, :]
    for i, j in enumerate(CONV_TAPS):
        conv = conv + shifted[i * CHUNK:(i + 1) * CHUNK, :] * cw_ref[j:j + 1, :]
    return _silu(conv)


def _ssd_body(refs, reverse, n_steps, cps, batch_id, step):
    if reverse:
        (act_in_ref, dt_ref, dtb_ref, alog_ref, yf_ref, z_ref, dskip_ref, gn_ref,
         y_ref, state_scr) = refs
    else:
        (xbc_ref, prev_ref, next_ref, dt_ref, cw_ref, cb_ref, dtb_ref, alog_ref,
         y_ref, act_out_ref, state_scr, shift_scr) = refs

    @pl.when(step == 0)
    def _():
        state_scr[...] = jnp.zeros_like(state_scr)

    if not reverse:
        @pl.when((batch_id == 0) & (step == 0))
        def _():
            _init_shift_matrix(shift_scr)

        halo_zeros = jnp.zeros((HALO_ROWS, D_XBC), BF16)
        prev = jnp.where(step > 0, prev_ref[...], halo_zeros)
        nxt = jnp.where(step < n_steps - 1, next_ref[...], halo_zeros)

    for ci in range(cps):
        c = (cps - 1 - ci) if reverse else ci
        rows = slice(c * CHUNK, (c + 1) * CHUNK)
        if reverse:
            act = act_in_ref[rows, :].astype(F32)
            extra = (yf_ref[rows, :], z_ref[rows, :], dskip_ref, gn_ref)
        else:
            lo = prev if c == 0 else xbc_ref[c * CHUNK - HALO_ROWS:c * CHUNK, :]
            hi = (nxt if c == cps - 1
                  else xbc_ref[(c + 1) * CHUNK:(c + 1) * CHUNK + HALO_ROWS, :])
            act = _conv_silu(lo, xbc_ref[rows, :], hi, shift_scr, cw_ref, cb_ref)
            act_out_ref[rows, :] = act.astype(act_out_ref.dtype)
            extra = None
        y = _ssd_chunk(act, dt_ref[rows, :], dtb_ref, alog_ref, state_scr, reverse, extra)
        y_ref[rows, :] = y.astype(y_ref.dtype)
        yield


def _ssd_chunk(act, dt_raw, dtb_ref, alog_ref, state_scr, reverse, extra):
    head_off = N_HEADS if reverse else 0
    last = 0 if reverse else CHUNK - 1
    xs = act[:, :D_SSD]
    b_all = act[:, D_SSD:D_SSD + SSD_GROUPS * D_STATE]
    c_all = act[:, D_SSD + SSD_GROUPS * D_STATE:].astype(BF16)

    pre_row = (dt_raw + dtb_ref[...]).T[:2 * N_HEADS]
    dt_row = _softplus(pre_row)
    dt_col = jnp.concatenate(
        [dt_row, jnp.zeros((LANES - 2 * N_HEADS, CHUNK), F32)], axis=0).T
    a_col = dt_col * (-jnp.exp(alog_ref[...]))
    r_io = lax.broadcasted_iota(jnp.int32, (CHUNK, CHUNK), 0)
    c_io = lax.broadcasted_iota(jnp.int32, (CHUNK, CHUNK), 1)
    mask = (c_io >= r_io) if reverse else (c_io <= r_io)
    tri = jnp.where(mask, 1.0, 0.0).astype(BF16)
    acum_col = jnp.zeros((CHUNK, LANES), F32)
    rest = a_col
    for _ in range(3):
        part = rest.astype(BF16)
        acum_col = acum_col + jnp.dot(tri, part, preferred_element_type=F32)
        rest = rest - part.astype(F32)
    acum2_col = acum_col * LOG2E
    acum2_row = acum2_col.T
    dt_row_b = dt_row.astype(BF16)

    lane = lax.broadcasted_iota(jnp.int32, (LANES, D_SSD), 1)
    rowi = lax.broadcasted_iota(jnp.int32, (LANES, D_SSD), 0)
    expand = jnp.where(rowi == head_off + lax.shift_right_logical(lane, HEAD_SHIFT),
                       1.0, 0.0).astype(BF16)
    w_col = dt_col * jnp.exp(acum_col[last:last + 1, :] - acum_col)
    stacked = jnp.concatenate([jnp.exp(acum_col), w_col], axis=0).astype(BF16)
    expanded = jnp.dot(stacked, expand, preferred_element_type=F32)
    ea_x, w_x = expanded[:CHUNK], expanded[CHUNK:]
    xs_b = xs.astype(BF16)
    xw_b = (xs * w_x).astype(BF16)

    state_in = state_scr[...]
    state_b = state_in.astype(BF16)

    lane2 = lax.broadcasted_iota(jnp.int32, (CHUNK, LANES), 1)
    low_half = lane2 < HEAD_DIM
    zero_b = jnp.zeros((CHUNK, LANES), BF16)

    heads_per_group = N_HEADS // SSD_GROUPS
    gw = heads_per_group * HEAD_DIM
    y_parts = []
    s_parts = []
    for g in range(SSD_GROUPS):
        bg_f = b_all[:, g * D_STATE:(g + 1) * D_STATE]
        bg = bg_f.astype(BF16)
        cg = c_all[:, g * D_STATE:(g + 1) * D_STATE]
        cb = jnp.einsum('ln,sn->ls', cg, bg, preferred_element_type=F32).astype(BF16)
        y_off = jnp.dot(cg, state_b[:, g * gw:(g + 1) * gw], preferred_element_type=F32)
        s_parts.append(jnp.dot(bg_f.T.astype(BF16), xw_b[:, g * gw:(g + 1) * gw],
                               preferred_element_type=F32))
        for p in range(heads_per_group // 2):
            h0 = g * heads_per_group + 2 * p
            ms = []
            for hh in (h0, h0 + 1):
                idx = head_off + hh
                seg = acum2_col[:, idx:idx + 1] - acum2_row[idx:idx + 1, :]
                decay = jnp.exp2(jnp.where(mask, seg, NEG_BIG)).astype(BF16)
                ms.append(decay * cb * dt_row_b[idx:idx + 1, :])
            lhs = jnp.concatenate(ms, axis=1)
            xp = xs_b[:, h0 * HEAD_DIM:h0 * HEAD_DIM + LANES]
            rhs = jnp.concatenate([jnp.where(low_half, xp, zero_b),
                                   jnp.where(low_half, zero_b, xp)], axis=0)
            y_diag = jnp.dot(lhs, rhs, preferred_element_type=F32)
            sl = slice(p * LANES, (p + 1) * LANES)
            y_parts.append(y_diag + y_off[:, sl] * ea_x[:, g * gw + p * LANES:g * gw + (p + 1) * LANES])

    state_scr[...] = state_in * ea_x[last:last + 1, :] + jnp.concatenate(s_parts, axis=1)
    y = jnp.concatenate(y_parts, axis=1)

    if reverse:
        yf, z, dskip_ref, gn_ref = extra
        y = y + yf.astype(F32) + dskip_ref[...] * xs
        y = y * _silu(z.astype(F32))
        ms2 = jnp.mean(y * y, axis=-1, keepdims=True)
        y = y * lax.rsqrt(ms2 + EPS) * gn_ref[...]
    return y


SSD_STEP_ROWS = SSD_CHUNKS_PER_STEP * CHUNK


def _ssd_fwd_plan(t_total, seq_len, small_shapes):
    rows = SSD_STEP_ROWS
    n_steps = seq_len // rows
    halo_per_step = rows // HALO_ROWS
    n_halo_blocks = t_total // HALO_ROWS
    sidx = lambda b, i: b * n_steps + i
    main = lambda b, i: (sidx(b, i), 0)
    prev = lambda b, i: (jnp.maximum(sidx(b, i) * halo_per_step - 1, 0), 0)
    nxt = lambda b, i: (jnp.minimum((sidx(b, i) + 1) * halo_per_step, n_halo_blocks - 1), 0)
    return dict(
        in_specs=[pl.BlockSpec((rows, D_XBC), main),
                  pl.BlockSpec((HALO_ROWS, D_XBC), prev),
                  pl.BlockSpec((HALO_ROWS, D_XBC), nxt),
                  pl.BlockSpec((rows, LANES), main)] + [_const_spec(s) for s in small_shapes],
        out_specs=[pl.BlockSpec((rows, D_SSD), main), pl.BlockSpec((rows, D_XBC), main)],
        out_shape=[jax.ShapeDtypeStruct((t_total, D_SSD), BF16),
                   jax.ShapeDtypeStruct((t_total, D_XBC), BF16)],
        scratch=[pltpu.VMEM((D_STATE, D_SSD), F32),
                 pltpu.VMEM((len(CONV_TAPS) * CHUNK, CONV_EXT), BF16)])


def _ssd_bwd_plan(t_total, seq_len, small_shapes):
    rows = SSD_STEP_ROWS
    n_steps = seq_len // rows
    main = lambda b, i: (b * n_steps + n_steps - 1 - i, 0)
    wide = pl.BlockSpec((rows, D_SSD), main)
    dtb_s, alog_s, dskip_s, gn_s = small_shapes
    return dict(
        in_specs=[pl.BlockSpec((rows, D_XBC), main), pl.BlockSpec((rows, LANES), main),
                  _const_spec(dtb_s), _const_spec(alog_s), wide, wide,
                  _const_spec(dskip_s), _const_spec(gn_s)],
        out_specs=[wide],
        out_shape=[jax.ShapeDtypeStruct((t_total, D_SSD), BF16)],
        scratch=[pltpu.VMEM((D_STATE, D_SSD), F32)])


N_ATTN_IN, N_ATTN_OUT, N_ATTN_SCR = 7, 2, 1


def _ssd_with_attn_kernel(*refs, reverse, n_steps, n_ssd, attn_parts):
    it = iter(refs)
    take = lambda n: [next(it) for _ in range(n)]
    ssd_in = take(n_ssd[0])
    attn_in = [take(N_ATTN_IN) for _ in attn_parts]
    ssd_out = take(n_ssd[1])
    attn_out = [take(N_ATTN_OUT) for _ in attn_parts]
    ssd_scr = take(n_ssd[2])
    attn_scr = [take(N_ATTN_SCR) for _ in attn_parts]
    batch_id, step = pl.program_id(0), pl.program_id(1)
    for scr in attn_scr:
        _band_attn_init(scr[-1], (batch_id == 0) & (step == 0))
    bodies = [_ssd_body(ssd_in + ssd_out + ssd_scr, reverse, n_steps, SSD_CHUNKS_PER_STEP,
                        batch_id, step)]
    for (tl, nl, n_res), a_in, a_out, a_scr in zip(attn_parts, attn_in, attn_out, attn_scr):
        bodies.append(_band_attn_main(a_in + a_out + a_scr, tl, nl, step % nl, n_res))
    for _ in itertools.chain(*bodies):
        pass


def _ssd_with_attn(ssd_plan, ssd_args, reverse, attn_inputs, batch, seq_len):
    n_steps = seq_len // SSD_STEP_ROWS
    parts, specs_in, specs_out, shapes_out, scratch, args, names = [], [], [], [], [], [], []
    for qd, kd, vd, dil in attn_inputs:
        nl = _attn_plan(qd.shape[0], seq_len, dil, 1, lambda b, i: (b, 0, 0))["n_lblocks"]
        n_res, rem = divmod(dil * nl, n_steps)
        assert rem == 0 and n_res >= 1 and (n_res == 1 or nl == 1), "attention must tile the grid"
        plan = _attn_plan(qd.shape[0], seq_len, dil, n_res,
                          lambda b, i, nl=nl: (b, i // nl, i % nl))
        parts.append((plan["tl"], nl, n_res))
        specs_in += plan["in_specs"]
        specs_out += plan["out_specs"]
        shapes_out += plan["out_shape"]
        scratch += plan["scratch"]
        args += [qd, kd, kd, kd, vd, vd, vd]
        names.append(f"d{dil}")
    n_ssd_out = len(ssd_plan["out_specs"])
    outs = pl.pallas_call(
        functools.partial(
            _ssd_with_attn_kernel, reverse=reverse, n_steps=n_steps,
            n_ssd=(len(ssd_plan["in_specs"]), n_ssd_out, len(ssd_plan["scratch"])),
            attn_parts=tuple(parts)),
        grid=(batch, n_steps),
        in_specs=ssd_plan["in_specs"] + specs_in,
        out_specs=ssd_plan["out_specs"] + specs_out,
        out_shape=ssd_plan["out_shape"] + shapes_out,
        scratch_shapes=ssd_plan["scratch"] + scratch,
        compiler_params=_cparams(("arbitrary", "arbitrary")),
        name=("ssd_bwd" if reverse else "ssd_fwd") + "_attn_" + "_".join(names),
    )(*ssd_args, *args)
    attn_outs = outs[n_ssd_out:]
    return outs[:n_ssd_out], [tuple(attn_outs[2 * i:2 * i + 2]) for i in range(len(parts))]


Q_SUB = 128
K_WIN = Q_SUB + 2 * BAND_RADIUS


def _band_attn_init(bias_scr, first_step):
    r = BAND_RADIUS

    @pl.when(first_step)
    def _():
        t_io = lax.broadcasted_iota(jnp.int32, (2 * Q_SUB, K_WIN), 0) & (Q_SUB - 1)
        u_io = lax.broadcasted_iota(jnp.int32, (2 * Q_SUB, K_WIN), 1)
        band = (u_io >= t_io) & (u_io <= t_io + 2 * r)
        bias_scr[0] = jnp.where(band, 0.0, NEG_BIG)
        bias_scr[1] = jnp.where(band & (u_io >= r), 0.0, NEG_BIG)
        bias_scr[2] = jnp.where(band & (u_io < K_WIN - r), 0.0, NEG_BIG)


def _band_attn_main(refs, tl, n_lblocks, lb, n_res):
    (q_ref, k_ref, kp_ref, kn_ref, v_ref, vp_ref, vn_ref,
     o_ref, stat_ref, bias_scr) = refs
    r = BAND_RADIUS

    def window(main_ref, prev_ref, next_ref, j, cols):
        lo = j * Q_SUB - r
        hi = lo + K_WIN
        if lo < 0:
            return jnp.concatenate([prev_ref[:, cols], main_ref[0:hi, cols]], axis=0)
        if hi > tl:
            return jnp.concatenate([main_ref[lo:tl, cols], next_ref[:, cols]], axis=0)
        return main_ref[lo:hi, cols]

    lane = lax.broadcasted_iota(jnp.int32, (Q_SUB, LANES), 1)
    low_half = lane < HEAD_DIM
    zero_b = jnp.zeros((Q_SUB, LANES), BF16)

    n_sub = tl // Q_SUB
    assert n_sub >= 2, "first and last score tiles of a block must differ"
    for res, j in itertools.product(range(n_res), range(n_sub)):
        if j == 0:
            slot = jnp.where(lb == 0, 1, 0)
        elif j == n_sub - 1:
            slot = jnp.where(lb == n_lblocks - 1, 2, 0)
        else:
            slot = 0
        rows = slice(j * Q_SUB, (j + 1) * Q_SUB)
        stat0 = res * LANES
        stat_ref[rows, stat0:stat0 + LANES] = jnp.zeros((Q_SUB, LANES), F32)
        for p in range(N_HEADS // 2):
            cols = slice(res * D_ATTN + p * LANES, res * D_ATTN + (p + 1) * LANES)
            qp = q_ref[rows, cols]
            qs = jnp.concatenate([jnp.where(low_half, qp, zero_b),
                                  jnp.where(low_half, zero_b, qp)], axis=0)
            kw = window(k_ref, kp_ref, kn_ref, j, cols)
            vw = window(v_ref, vp_ref, vn_ref, j, cols)
            s = jnp.einsum('qd,kd->qk', qs, kw, preferred_element_type=F32) + bias_scr[slot]
            m = jnp.max(s, axis=-1, keepdims=True)
            e = jnp.exp2(s - m)
            l = jnp.sum(e, axis=-1, keepdims=True)
            o2 = jnp.dot(e.astype(BF16), vw, preferred_element_type=F32)
            o_ref[rows, cols] = jnp.where(low_half, o2[:Q_SUB], o2[Q_SUB:]).astype(o_ref.dtype)
            for hh, sl in ((2 * p, slice(0, Q_SUB)), (2 * p + 1, slice(Q_SUB, 2 * Q_SUB))):
                stat_ref[rows, stat0 + hh:stat0 + hh + 1] = m[sl]
                stat_ref[rows, stat0 + N_HEADS + hh:stat0 + N_HEADS + hh + 1] = l[sl]
        yield


def _attn_plan(n_rows, seq_len, dil, n_res, to_bcl):
    sub_len = seq_len // dil
    tl = min(SSD_STEP_ROWS, sub_len)
    n_lblocks = sub_len // tl
    r = BAND_RADIUS
    halo_per_block = tl // r
    n_halo = n_rows // r
    width = n_res * D_ATTN

    def main(*ids):
        b, c, l = to_bcl(*ids)
        return (b * n_lblocks + l, c)

    def prev(*ids):
        b, c, l = to_bcl(*ids)
        return (jnp.maximum((b * n_lblocks + l) * halo_per_block - 1, 0), c)

    def nxt(*ids):
        b, c, l = to_bcl(*ids)
        return (jnp.minimum((b * n_lblocks + l + 1) * halo_per_block, n_halo - 1), c)

    blk = pl.BlockSpec((tl, width), main)
    hp = pl.BlockSpec((r, width), prev)
    hn = pl.BlockSpec((r, width), nxt)
    return dict(
        tl=tl, n_lblocks=n_lblocks,
        in_specs=[blk, blk, hp, hn, blk, hp, hn],
        out_specs=[blk, pl.BlockSpec((tl, n_res * LANES), main)],
        out_shape=[jax.ShapeDtypeStruct((n_rows, dil * D_ATTN), BF16),
                   jax.ShapeDtypeStruct((n_rows, dil * LANES), F32)],
        scratch=[pltpu.VMEM((3, 2 * Q_SUB, K_WIN), F32)])


def _out_proj_kernel(x_ref, ys_ref, o1_ref, o2_ref, o3_ref, l1_ref, l2_ref, l3_ref,
                     za_ref, wo_ref, gf_ref, y_ref, o_scr, omid_scr, l_scr, lmid_scr):
    tm = x_ref.shape[0]

    def token_order(src_ref, scr, mid, dil, width):
        for j in range(width // LANES):
            if dil == SUB_DIL:
                for c in range(dil):
                    col = c * width + j * LANES
                    scr[j, pl.ds(c, tm // dil, stride=dil), :] = (
                        src_ref[:, col:col + LANES].astype(F32))
            else:
                for c4 in range(SUB_DIL):
                    for c2 in range(SUB_DIL):
                        col = (c4 + SUB_DIL * c2) * width + j * LANES
                        mid[j, c4, pl.ds(c2, tm // dil, stride=SUB_DIL), :] = (
                            src_ref[:, col:col + LANES].astype(F32))
                for c4 in range(SUB_DIL):
                    scr[j, pl.ds(c4, tm // SUB_DIL, stride=SUB_DIL), :] = mid[j, c4]
        return jnp.concatenate([scr[j] for j in range(width // LANES)], axis=1)

    s1 = l1_ref[...]
    s2 = token_order(l2_ref, l_scr.at[0], lmid_scr, DILATIONS[1], LANES)
    s3 = token_order(l3_ref, l_scr.at[1], lmid_scr, DILATIONS[2], LANES)
    m = jnp.maximum(jnp.maximum(s1, s2), s3)
    head_lane = lax.broadcasted_iota(jnp.int32, s1.shape, 1) < N_HEADS
    a = [jnp.exp2(s - m) for s in (s1, s2, s3)]
    den = sum(ap * pltpu.roll(s, LANES - N_HEADS, 1) for ap, s in zip(a, (s1, s2, s3)))
    inv = jnp.where(head_lane, 1.0 / jnp.where(head_lane, den, 1.0), 0.0)
    lane = lax.broadcasted_iota(jnp.int32, (LANES, D_ATTN), 1)
    rowi = lax.broadcasted_iota(jnp.int32, (LANES, D_ATTN), 0)
    expand = jnp.where(rowi == lax.shift_right_logical(lane, HEAD_SHIFT), 1.0, 0.0).astype(BF16)

    def weight(ap):
        return jnp.dot((ap * inv).astype(BF16), expand, preferred_element_type=F32)

    mix = weight(a[0]) * o1_ref[...].astype(F32)
    mix = mix + weight(a[1]) * token_order(o2_ref, o_scr.at[0], omid_scr, DILATIONS[1], D_ATTN)
    mix = mix + weight(a[2]) * token_order(o3_ref, o_scr.at[1], omid_scr, DILATIONS[2], D_ATTN)
    y_attn = (mix * _silu(za_ref[...].astype(F32))).astype(BF16)
    out = x_ref[...]
    out = out + jnp.dot(ys_ref[...], wo_ref[:D_SSD, :], preferred_element_type=F32)
    out = out + jnp.dot(y_attn, wo_ref[D_SSD:, :], preferred_element_type=F32)
    ms = jnp.mean(out * out, axis=-1, keepdims=True)
    y_ref[...] = out * lax.rsqrt(ms + EPS) * gf_ref[...]


def _out_proj(x2, y_ssd, os_, lses, za, w_out, norm_f, tm):
    t_total = x2.shape[0]
    row = lambda i: (i, 0)
    wide = pl.BlockSpec((tm, D_MODEL), row)
    o_specs = [pl.BlockSpec((tm // d, d * D_ATTN), row) for d in DILATIONS]
    l_specs = [pl.BlockSpec((tm // d, d * LANES), row) for d in DILATIONS]
    return pl.pallas_call(
        _out_proj_kernel,
        grid=(t_total // tm,),
        in_specs=[wide, wide] + o_specs + l_specs
                 + [wide, _const_spec(w_out.shape), _const_spec(norm_f.shape)],
        out_specs=wide,
        out_shape=jax.ShapeDtypeStruct((t_total, D_MODEL), F32),
        scratch_shapes=[pltpu.VMEM((2, D_ATTN // LANES, tm, LANES), F32),
                        pltpu.VMEM((D_ATTN // LANES, SUB_DIL, tm // SUB_DIL, LANES), F32),
                        pltpu.VMEM((2, 1, tm, LANES), F32),
                        pltpu.VMEM((1, SUB_DIL, tm // SUB_DIL, LANES), F32)],
        compiler_params=_cparams(("arbitrary",)),
        name="out_proj",
    )(x2, y_ssd, *os_, *lses, za, w_out, norm_f)


def _rope_tables(seq_len):
    half = HEAD_DIM // 2
    inv = ROPE_THETA ** (-jnp.arange(half, dtype=F32) * 2.0 / HEAD_DIM)
    ang = jnp.arange(seq_len, dtype=F32)[:, None] * inv[None, :]
    cos, sin = jnp.cos(ang), jnp.sin(ang)
    cos_h = jnp.concatenate([cos, cos], axis=-1)
    sin_h = jnp.concatenate([-sin, sin], axis=-1)
    reps = LANES // HEAD_DIM
    return jnp.tile(cos_h, (1, reps)), jnp.tile(sin_h, (1, reps))


def _pad_lanes(v, offset=0):
    out = jnp.zeros((1, LANES), F32)
    return out.at[0, offset:offset + v.shape[0]].set(v.astype(F32))


def _encoder(x, tables, params):
    (ws, conv_w, conv_b, dtb, alog, dskip_x, norm_ssd, norm_in, w_out, norm_f) = params
    batch, seq_len, _ = x.shape
    t_total = batch * seq_len
    x2 = x.reshape(t_total, D_MODEL)
    cos, sin = tables
    q, k, v, za, zs, xbc, dt = _in_proj(x2, norm_in, cos[:seq_len], sin[:seq_len], ws,
                                        seq_len, tm=IN_PROJ_ROWS)
    fwd_plan = _ssd_fwd_plan(t_total, seq_len,
                             (conv_w.shape, conv_b.shape, dtb.shape, alog.shape))
    (y_f, act), (attn4,) = _ssd_with_attn(
        fwd_plan, (xbc, xbc, xbc, dt, conv_w, conv_b, dtb, alog), False,
        [(q[1], k[1], v[1], DILATIONS[1])], batch, seq_len)
    bwd_plan = _ssd_bwd_plan(t_total, seq_len,
                             (dtb.shape, alog.shape, dskip_x.shape, norm_ssd.shape))
    (y_ssd,), (attn1, attn16) = _ssd_with_attn(
        bwd_plan, (act, dt, dtb, alog, y_f, zs, dskip_x, norm_ssd), True,
        [(q[0], k[0], v[0], DILATIONS[0]), (q[2], k[2], v[2], DILATIONS[2])], batch, seq_len)
    os_, stats = zip(attn1, attn4, attn16)
    y = _out_proj(x2, y_ssd, os_, stats, za, w_out, norm_f, tm=OUT_PROJ_ROWS)
    return y.reshape(batch, seq_len, D_MODEL)


def kernel(x_prompt, x_sample, w_in, conv_w, conv_b, dt_bias_fwd, dt_bias_bwd, a_log_fwd,
           a_log_bwd, d_skip, norm_ssd, norm_in, w_out, norm_f):
    assert w_in.shape[0] == 1, "single-layer encoder"
    w = w_in[0]
    bounds = [0, D_ATTN, 2 * D_ATTN, 3 * D_ATTN, 4 * D_ATTN, 4 * D_ATTN + D_SSD,
              4 * D_ATTN + D_SSD + D_XBC]
    segs = [w[:, bounds[i]:bounds[i + 1]].astype(BF16) for i in range(6)]
    w_dt = jnp.zeros((D_MODEL, LANES), F32).at[:, :2 * N_HEADS].set(w[:, bounds[6]:])
    ws = tuple(segs) + (w_dt.astype(BF16),)
    dtb = _pad_lanes(dt_bias_fwd[0]) + _pad_lanes(dt_bias_bwd[0], N_HEADS)
    alog = _pad_lanes(a_log_fwd[0]) + _pad_lanes(a_log_bwd[0], N_HEADS)
    dskip_x = jnp.repeat(d_skip[0].astype(F32), HEAD_DIM)[None, :]
    params = (ws, conv_w[0].astype(F32), conv_b[0].astype(F32)[None, :], dtb, alog, dskip_x,
              norm_ssd[0].astype(F32)[None, :], norm_in[0].astype(F32)[None, :],
              w_out[0].astype(BF16), norm_f.astype(F32)[None, :])
    tables = _rope_tables(max(x_prompt.shape[1], x_sample.shape[1]))
    return (_encoder(x_prompt, tables, params), _encoder(x_sample, tables, params))
```

```python
import functools
import itertools

import jax
import jax.numpy as jnp
from jax import lax
from jax.experimental import pallas as pl
from jax.experimental.pallas import tpu as pltpu

D_MODEL = 1024
D_ATTN = 1024
D_SSD = 1024
HEAD_DIM = 64
HEAD_SHIFT = 6
N_HEADS = 16
D_STATE = 128
SSD_GROUPS = 2
CONV_WIDTH = 5
CHUNK = 128
ROPE_THETA = 10000.0
EPS = 1e-5
DILATIONS = (1, 4, 16)
SUB_DIL = 4
BAND_RADIUS = 64
D_XBC = D_SSD + 2 * SSD_GROUPS * D_STATE
LANES = 128
HALO_ROWS = 16
NEG_BIG = -1e30
LOG2E = 1.4426950408889634
Q_SCALE = HEAD_DIM ** -0.5 * LOG2E

F32 = jnp.float32
BF16 = jnp.bfloat16

VMEM_LIMIT = 56 * 1024 * 1024
IN_PROJ_ROWS = 512
EMIT_SLOTS = 6
OUT_PROJ_ROWS = 512
SSD_CHUNKS_PER_STEP = 4


def _cparams(sem):
    return pltpu.CompilerParams(dimension_semantics=sem, vmem_limit_bytes=VMEM_LIMIT)


def _const_spec(shape):
    nd = len(shape)
    return pl.BlockSpec(shape, lambda *_: (0,) * nd)


def _in_proj_kernel(x_ref, g_ref, cos_ref, sin_ref, wq_ref, wk_ref, wv_ref, wza_ref,
                    wzs_ref, wxbc_ref, wdt_ref,
                    q1_ref, q4_ref, q16_ref, k1_ref, k4_ref, k16_ref, v1_ref, v4_ref, v16_ref,
                    za_ref, zs_ref, xbc_ref, dt_ref, blk_scr, sub_scr):
    tm = x_ref.shape[0]

    def emit_dilated(t, j, outs):
        o1_ref, o4_ref, o16_ref = outs
        o1_ref[:, j * LANES:(j + 1) * LANES] = t.astype(BF16)
        slot = next(slots)
        blk, sub = blk_scr.at[slot], sub_scr.at[slot]
        blk[...] = t
        for c4 in range(SUB_DIL):
            a = blk[pl.ds(c4, tm // SUB_DIL, stride=SUB_DIL), :]
            col = c4 * D_ATTN + j * LANES
            o4_ref[:, col:col + LANES] = a.astype(BF16)
            sub[c4] = a
        for c4 in range(SUB_DIL):
            for c2 in range(SUB_DIL):
                col = (c4 + SUB_DIL * c2) * D_ATTN + j * LANES
                o16_ref[:, col:col + LANES] = sub[
                    c4, pl.ds(c2, tm // (SUB_DIL * SUB_DIL), stride=SUB_DIL), :].astype(BF16)

    slots = itertools.cycle(range(blk_scr.shape[0]))

    x = x_ref[...]
    ms = jnp.mean(x * x, axis=-1, keepdims=True)
    h = (x * lax.rsqrt(ms + EPS) * g_ref[...]).astype(BF16)

    cos = cos_ref[...]
    sin = sin_ref[...]
    lane = lax.broadcasted_iota(jnp.int32, cos.shape, 1)
    first_half = (lane & (HEAD_DIM - 1)) < (HEAD_DIM // 2)

    def rotary(t):
        partner = jnp.where(first_half,
                            pltpu.roll(t, LANES - HEAD_DIM // 2, 1),
                            pltpu.roll(t, HEAD_DIM // 2, 1))
        return t * cos + partner * sin

    q_outs = (q1_ref, q4_ref, q16_ref)
    k_outs = (k1_ref, k4_ref, k16_ref)
    v_outs = (v1_ref, v4_ref, v16_ref)
    mxu_cols = 2 * LANES
    for jj in range(D_ATTN // mxu_cols):
        sl = slice(jj * mxu_cols, (jj + 1) * mxu_cols)
        tq = jnp.dot(h, wq_ref[:, sl], preferred_element_type=F32)
        tk = jnp.dot(h, wk_ref[:, sl], preferred_element_type=F32)
        tv = jnp.dot(h, wv_ref[:, sl], preferred_element_type=F32)
        for half in range(2):
            hs = slice(half * LANES, (half + 1) * LANES)
            j = 2 * jj + half
            emit_dilated(rotary(tq[:, hs]) * Q_SCALE, j, q_outs)
            emit_dilated(rotary(tk[:, hs]), j, k_outs)
            emit_dilated(tv[:, hs], j, v_outs)

    for w_ref, o_ref in ((wza_ref, za_ref), (wzs_ref, zs_ref),
                         (wxbc_ref, xbc_ref), (wdt_ref, dt_ref)):
        o_ref[...] = jnp.dot(h, w_ref[...],
                             preferred_element_type=F32).astype(o_ref.dtype)


def _in_proj(x2, norm_in, cos, sin, ws, seq_len, tm):
    t_total = x2.shape[0]
    blocks_per_seq = seq_len // tm
    row = lambda i: (i, 0)
    pos = lambda i: (i % blocks_per_seq, 0)
    attn_shapes = [jax.ShapeDtypeStruct((t_total // d, d * D_ATTN), BF16) for d in DILATIONS]
    attn_specs = [pl.BlockSpec((tm // d, d * D_ATTN), row) for d in DILATIONS]
    out_shapes = (attn_shapes * 3
                  + [jax.ShapeDtypeStruct((t_total, D_ATTN), BF16)] * 2
                  + [jax.ShapeDtypeStruct((t_total, D_XBC), BF16),
                     jax.ShapeDtypeStruct((t_total, LANES), F32)])
    weight_specs = [pl.BlockSpec(w.shape, lambda i: (0, 0), pipeline_mode=pl.Buffered(1))
                    for w in ws]
    outs = pl.pallas_call(
        _in_proj_kernel,
        grid=(t_total // tm,),
        in_specs=[pl.BlockSpec((tm, D_MODEL), row), _const_spec((1, D_MODEL)),
                  pl.BlockSpec((tm, LANES), pos), pl.BlockSpec((tm, LANES), pos)]
                 + weight_specs,
        out_specs=attn_specs * 3
                  + [pl.BlockSpec((tm, D_ATTN), row)] * 2
                  + [pl.BlockSpec((tm, D_XBC), row), pl.BlockSpec((tm, LANES), row)],
        out_shape=out_shapes,
        scratch_shapes=[pltpu.VMEM((EMIT_SLOTS, tm, LANES), F32),
                        pltpu.VMEM((EMIT_SLOTS, SUB_DIL, tm // SUB_DIL, LANES), F32)],
        compiler_params=_cparams(("arbitrary",)),
        name="in_proj",
    )(x2, norm_in, cos, sin, *ws)
    q, k, v = outs[0:3], outs[3:6], outs[6:9]
    return q, k, v, outs[9], outs[10], outs[11], outs[12]


def _silu(x):
    h = 0.5 * x
    return h + h * jnp.tanh(h)


def _softplus(x):
    return jnp.maximum(x, 0.0) + jnp.log(1.0 + jnp.exp(-jnp.abs(x)))


CONV_PAD = (CONV_WIDTH - 1) // 2
CONV_EXT = 2 * CHUNK
CONV_LEAD = (CONV_EXT - CHUNK) // 2
CONV_TAPS = tuple(j for j in range(CONV_WIDTH) if j != CONV_PAD)


def _init_shift_matrix(shift_scr):
    shape = (len(CONV_TAPS) * CHUNK, CONV_EXT)
    row = lax.broadcasted_iota(jnp.int32, shape, 0)
    col = lax.broadcasted_iota(jnp.int32, shape, 1)
    t = row & (CHUNK - 1)
    blk = lax.shift_right_logical(row, CHUNK.bit_length() - 1)
    off = jnp.zeros(shape, jnp.int32)
    for i, j in enumerate(CONV_TAPS):
        off = jnp.where(blk == i, j - CONV_PAD, off)
    shift_scr[...] = jnp.where(col == CONV_LEAD + t + off, 1.0, 0.0).astype(BF16)


def _conv_silu(lo, mid, hi, shift_scr, cw_ref, cb_ref):
    zeros = jnp.zeros((CONV_LEAD - HALO_ROWS, D_XBC), BF16)
    window = jnp.concatenate([zeros, lo, mid, hi, zeros], axis=0)
    shifted = jnp.dot(shift_scr[...], window, preferred_element_type=F32)
    conv = cb_ref[...] + mid.astype(F32) * cw_ref[CONV_PAD:CONV_PAD + 1, :]
    for i, j in enumerate(CONV_TAPS):
        conv = conv + shifted[i * CHUNK:(i + 1) * CHUNK, :] * cw_ref[j:j + 1, :]
    return _silu(conv)


def _ssd_body(refs, reverse, n_steps, cps, batch_id, step):
    if reverse:
        (act_in_ref, dt_ref, dtb_ref, alog_ref, yf_ref, z_ref, dskip_ref, gn_ref,
         y_ref, state_scr) = refs
    else:
        (xbc_ref, prev_ref, next_ref, dt_ref, cw_ref, cb_ref, dtb_ref, alog_ref,
         y_ref, act_out_ref, state_scr, shift_scr) = refs

    @pl.when(step == 0)
    def _():
        state_scr[...] = jnp.zeros_like(state_scr)

    if not reverse:
        @pl.when((batch_id == 0) & (step == 0))
        def _():
            _init_shift_matrix(shift_scr)

        halo_zeros = jnp.zeros((HALO_ROWS, D_XBC), BF16)
        prev = jnp.where(step > 0, prev_ref[...], halo_zeros)
        nxt = jnp.where(step < n_steps - 1, next_ref[...], halo_zeros)

    for ci in range(cps):
        c = (cps - 1 - ci) if reverse else ci
        rows = slice(c * CHUNK, (c + 1) * CHUNK)
        if reverse:
            act = act_in_ref[rows, :].astype(F32)
            extra = (yf_ref[rows, :], z_ref[rows, :], dskip_ref, gn_ref)
        else:
            lo = prev if c == 0 else xbc_ref[c * CHUNK - HALO_ROWS:c * CHUNK, :]
            hi = (nxt if c == cps - 1
                  else xbc_ref[(c + 1) * CHUNK:(c + 1) * CHUNK + HALO_ROWS, :])
            act = _conv_silu(lo, xbc_ref[rows, :], hi, shift_scr, cw_ref, cb_ref)
            act_out_ref[rows, :] = act.astype(act_out_ref.dtype)
            extra = None
        y = _ssd_chunk(act, dt_ref[rows, :], dtb_ref, alog_ref, state_scr, reverse, extra)
        y_ref[rows, :] = y.astype(y_ref.dtype)
        yield


def _ssd_chunk(act, dt_raw, dtb_ref, alog_ref, state_scr, reverse, extra):
    head_off = N_HEADS if reverse else 0
    last = 0 if reverse else CHUNK - 1
    xs = act[:, :D_SSD]
    b_all = act[:, D_SSD:D_SSD + SSD_GROUPS * D_STATE]
    c_all = act[:, D_SSD + SSD_GROUPS * D_STATE:].astype(BF16)

    pre_row = (dt_raw + dtb_ref[...]).T[:2 * N_HEADS]
    dt_row = _softplus(pre_row)
    dt_col = jnp.concatenate(
        [dt_row, jnp.zeros((LANES - 2 * N_HEADS, CHUNK), F32)], axis=0).T
    a_col = dt_col * (-jnp.exp(alog_ref[...]))
    r_io = lax.broadcasted_iota(jnp.int32, (CHUNK, CHUNK), 0)
    c_io = lax.broadcasted_iota(jnp.int32, (CHUNK, CHUNK), 1)
    mask = (c_io >= r_io) if reverse else (c_io <= r_io)
    tri = jnp.where(mask, 1.0, 0.0).astype(BF16)
    acum_col = jnp.zeros((CHUNK, LANES), F32)
    rest = a_col
    for _ in range(3):
        part = rest.astype(BF16)
        acum_col = acum_col + jnp.dot(tri, part, preferred_element_type=F32)
        rest = rest - part.astype(F32)
    acum2_col = acum_col * LOG2E
    acum2_row = acum2_col.T
    dt_row_b = dt_row.astype(BF16)

    lane = lax.broadcasted_iota(jnp.int32, (LANES, D_SSD), 1)
    rowi = lax.broadcasted_iota(jnp.int32, (LANES, D_SSD), 0)
    expand = jnp.where(rowi == head_off + lax.shift_right_logical(lane, HEAD_SHIFT),
                       1.0, 0.0).astype(BF16)
    w_col = dt_col * jnp.exp(acum_col[last:last + 1, :] - acum_col)
    stacked = jnp.concatenate([jnp.exp(acum_col), w_col], axis=0).astype(BF16)
    expanded = jnp.dot(stacked, expand, preferred_element_type=F32)
    ea_x, w_x = expanded[:CHUNK], expanded[CHUNK:]
    xs_b = xs.astype(BF16)
    xw_b = (xs * w_x).astype(BF16)

    state_in = state_scr[...]
    state_b = state_in.astype(BF16)

    lane2 = lax.broadcasted_iota(jnp.int32, (CHUNK, LANES), 1)
    low_half = lane2 < HEAD_DIM
    zero_b = jnp.zeros((CHUNK, LANES), BF16)

    heads_per_group = N_HEADS // SSD_GROUPS
    gw = heads_per_group * HEAD_DIM
    y_parts = []
    s_parts = []
    for g in range(SSD_GROUPS):
        bg_f = b_all[:, g * D_STATE:(g + 1) * D_STATE]
        bg = bg_f.astype(BF16)
        cg = c_all[:, g * D_STATE:(g + 1) * D_STATE]
        cb = jnp.einsum('ln,sn->ls', cg, bg, preferred_element_type=F32).astype(BF16)
        y_off = jnp.dot(cg, state_b[:, g * gw:(g + 1) * gw], preferred_element_type=F32)
        s_parts.append(jnp.dot(bg_f.T.astype(BF16), xw_b[:, g * gw:(g + 1) * gw],
                               preferred_element_type=F32))
        for p in range(heads_per_group // 2):
            h0 = g * heads_per_group + 2 * p
            ms = []
            for hh in (h0, h0 + 1):
                idx = head_off + hh
                seg = acum2_col[:, idx:idx + 1] - acum2_row[idx:idx + 1, :]
                decay = jnp.exp2(jnp.where(mask, seg, NEG_BIG)).astype(BF16)
                ms.append(decay * cb * dt_row_b[idx:idx + 1, :])
            lhs = jnp.concatenate(ms, axis=1)
            xp = xs_b[:, h0 * HEAD_DIM:h0 * HEAD_DIM + LANES]
            rhs = jnp.concatenate([jnp.where(low_half, xp, zero_b),
                                   jnp.where(low_half, zero_b, xp)], axis=0)
            y_diag = jnp.dot(lhs, rhs, preferred_element_type=F32)
            sl = slice(p * LANES, (p + 1) * LANES)
            y_parts.append(y_diag + y_off[:, sl] * ea_x[:, g * gw + p * LANES:g * gw + (p + 1) * LANES])

    state_scr[...] = state_in * ea_x[last:last + 1, :] + jnp.concatenate(s_parts, axis=1)
    y = jnp.concatenate(y_parts, axis=1)

    if reverse:
        yf, z, dskip_ref, gn_ref = extra
        y = y + yf.astype(F32) + dskip_ref[...] * xs
        y = y * _silu(z.astype(F32))
        ms2 = jnp.mean(y * y, axis=-1, keepdims=True)
        y = y * lax.rsqrt(ms2 + EPS) * gn_ref[...]
    return y


SSD_STEP_ROWS = SSD_CHUNKS_PER_STEP * CHUNK


def _ssd_fwd_plan(t_total, seq_len, small_shapes):
    rows = SSD_STEP_ROWS
    n_steps = seq_len // rows
    halo_per_step = rows // HALO_ROWS
    n_halo_blocks = t_total // HALO_ROWS
    sidx = lambda b, i: b * n_steps + i
    main = lambda b, i: (sidx(b, i), 0)
    prev = lambda b, i: (jnp.maximum(sidx(b, i) * halo_per_step - 1, 0), 0)
    nxt = lambda b, i: (jnp.minimum((sidx(b, i) + 1) * halo_per_step, n_halo_blocks - 1), 0)
    return dict(
        in_specs=[pl.BlockSpec((rows, D_XBC), main),
                  pl.BlockSpec((HALO_ROWS, D_XBC), prev),
                  pl.BlockSpec((HALO_ROWS, D_XBC), nxt),
                  pl.BlockSpec((rows, LANES), main)] + [_const_spec(s) for s in small_shapes],
        out_specs=[pl.BlockSpec((rows, D_SSD), main), pl.BlockSpec((rows, D_XBC), main)],
        out_shape=[jax.ShapeDtypeStruct((t_total, D_SSD), BF16),
                   jax.ShapeDtypeStruct((t_total, D_XBC), BF16)],
        scratch=[pltpu.VMEM((D_STATE, D_SSD), F32),
                 pltpu.VMEM((len(CONV_TAPS) * CHUNK, CONV_EXT), BF16)])


def _ssd_bwd_plan(t_total, seq_len, small_shapes):
    rows = SSD_STEP_ROWS
    n_steps = seq_len // rows
    main = lambda b, i: (b * n_steps + n_steps - 1 - i, 0)
    wide = pl.BlockSpec((rows, D_SSD), main)
    dtb_s, alog_s, dskip_s, gn_s = small_shapes
    return dict(
        in_specs=[pl.BlockSpec((rows, D_XBC), main), pl.BlockSpec((rows, LANES), main),
                  _const_spec(dtb_s), _const_spec(alog_s), wide, wide,
                  _const_spec(dskip_s), _const_spec(gn_s)],
        out_specs=[wide],
        out_shape=[jax.ShapeDtypeStruct((t_total, D_SSD), BF16)],
        scratch=[pltpu.VMEM((D_STATE, D_SSD), F32)])


N_ATTN_IN, N_ATTN_OUT, N_ATTN_SCR = 7, 2, 1


def _ssd_with_attn_kernel(*refs, reverse, n_steps, n_ssd, attn_parts):
    it = iter(refs)
    take = lambda n: [next(it) for _ in range(n)]
    ssd_in = take(n_ssd[0])
    attn_in = [take(N_ATTN_IN) for _ in attn_parts]
    ssd_out = take(n_ssd[1])
    attn_out = [take(N_ATTN_OUT) for _ in attn_parts]
    ssd_scr = take(n_ssd[2])
    attn_scr = [take(N_ATTN_SCR) for _ in attn_parts]
    batch_id, step = pl.program_id(0), pl.program_id(1)
    for scr in attn_scr:
        _band_attn_init(scr[-1], (batch_id == 0) & (step == 0))
    bodies = [_ssd_body(ssd_in + ssd_out + ssd_scr, reverse, n_steps, SSD_CHUNKS_PER_STEP,
                        batch_id, step)]
    for (tl, nl, n_res), a_in, a_out, a_scr in zip(attn_parts, attn_in, attn_out, attn_scr):
        bodies.append(_band_attn_main(a_in + a_out + a_scr, tl, nl, step % nl, n_res))
    for _ in itertools.chain(*bodies):
        pass


def _ssd_with_attn(ssd_plan, ssd_args, reverse, attn_inputs, batch, seq_len):
    n_steps = seq_len // SSD_STEP_ROWS
    parts, specs_in, specs_out, shapes_out, scratch, args, names = [], [], [], [], [], [], []
    for qd, kd, vd, dil in attn_inputs:
        nl = _attn_plan(qd.shape[0], seq_len, dil, 1, lambda b, i: (b, 0, 0))["n_lblocks"]
        n_res, rem = divmod(dil * nl, n_steps)
        assert rem == 0 and n_res >= 1 and (n_res == 1 or nl == 1), "attention must tile the grid"
        plan = _attn_plan(qd.shape[0], seq_len, dil, n_res,
                          lambda b, i, nl=nl: (b, i // nl, i % nl))
        parts.append((plan["tl"], nl, n_res))
        specs_in += plan["in_specs"]
        specs_out += plan["out_specs"]
        shapes_out += plan["out_shape"]
        scratch += plan["scratch"]
        args += [qd, kd, kd, kd, vd, vd, vd]
        names.append(f"d{dil}")
    n_ssd_out = len(ssd_plan["out_specs"])
    outs = pl.pallas_call(
        functools.partial(
            _ssd_with_attn_kernel, reverse=reverse, n_steps=n_steps,
            n_ssd=(len(ssd_plan["in_specs"]), n_ssd_out, len(ssd_plan["scratch"])),
            attn_parts=tuple(parts)),
        grid=(batch, n_steps),
        in_specs=ssd_plan["in_specs"] + specs_in,
        out_specs=ssd_plan["out_specs"] + specs_out,
        out_shape=ssd_plan["out_shape"] + shapes_out,
        scratch_shapes=ssd_plan["scratch"] + scratch,
        compiler_params=_cparams(("arbitrary", "arbitrary")),
        name=("ssd_bwd" if reverse else "ssd_fwd") + "_attn_" + "_".join(names),
    )(*ssd_args, *args)
    attn_outs = outs[n_ssd_out:]
    return outs[:n_ssd_out], [tuple(attn_outs[2 * i:2 * i + 2]) for i in range(len(parts))]


Q_SUB = 128
K_WIN = Q_SUB + 2 * BAND_RADIUS


def _band_attn_init(bias_scr, first_step):
    r = BAND_RADIUS

    @pl.when(first_step)
    def _():
        t_io = lax.broadcasted_iota(jnp.int32, (2 * Q_SUB, K_WIN), 0) & (Q_SUB - 1)
        u_io = lax.broadcasted_iota(jnp.int32, (2 * Q_SUB, K_WIN), 1)
        band = (u_io >= t_io) & (u_io <= t_io + 2 * r)
        bias_scr[0] = jnp.where(band, 0.0, NEG_BIG)
        bias_scr[1] = jnp.where(band & (u_io >= r), 0.0, NEG_BIG)
        bias_scr[2] = jnp.where(band & (u_io < K_WIN - r), 0.0, NEG_BIG)


def _band_attn_main(refs, tl, n_lblocks, lb, n_res):
    (q_ref, k_ref, kp_ref, kn_ref, v_ref, vp_ref, vn_ref,
     o_ref, stat_ref, bias_scr) = refs
    r = BAND_RADIUS

    def window(main_ref, prev_ref, next_ref, j, cols):
        lo = j * Q_SUB - r
        hi = lo + K_WIN
        if lo < 0:
            return jnp.concatenate([prev_ref[:, cols], main_ref[0:hi, cols]], axis=0)
        if hi > tl:
            return jnp.concatenate([main_ref[lo:tl, cols], next_ref[:, cols]], axis=0)
        return main_ref[lo:hi, cols]

    lane = lax.broadcasted_iota(jnp.int32, (Q_SUB, LANES), 1)
    low_half = lane < HEAD_DIM
    zero_b = jnp.zeros((Q_SUB, LANES), BF16)

    n_sub = tl // Q_SUB
    assert n_sub >= 2, "first and last score tiles of a block must differ"
    for res, j in itertools.product(range(n_res), range(n_sub)):
        if j == 0:
            slot = jnp.where(lb == 0, 1, 0)
        elif j == n_sub - 1:
            slot = jnp.where(lb == n_lblocks - 1, 2, 0)
        else:
            slot = 0
        rows = slice(j * Q_SUB, (j + 1) * Q_SUB)
        stat0 = res * LANES
        stat_ref[rows, stat0:stat0 + LANES] = jnp.zeros((Q_SUB, LANES), F32)
        for p in range(N_HEADS // 2):
            cols = slice(res * D_ATTN + p * LANES, res * D_ATTN + (p + 1) * LANES)
            qp = q_ref[rows, cols]
            qs = jnp.concatenate([jnp.where(low_half, qp, zero_b),
                                  jnp.where(low_half, zero_b, qp)], axis=0)
            kw = window(k_ref, kp_ref, kn_ref, j, cols)
            vw = window(v_ref, vp_ref, vn_ref, j, cols)
            s = jnp.einsum('qd,kd->qk', qs, kw, preferred_element_type=F32) + bias_scr[slot]
            m = jnp.max(s, axis=-1, keepdims=True)
            e = jnp.exp2(s - m)
            l = jnp.sum(e, axis=-1, keepdims=True)
            o2 = jnp.dot(e.astype(BF16), vw, preferred_element_type=F32)
            o_ref[rows, cols] = jnp.where(low_half, o2[:Q_SUB], o2[Q_SUB:]).astype(o_ref.dtype)
            for hh, sl in ((2 * p, slice(0, Q_SUB)), (2 * p + 1, slice(Q_SUB, 2 * Q_SUB))):
                stat_ref[rows, stat0 + hh:stat0 + hh + 1] = m[sl]
                stat_ref[rows, stat0 + N_HEADS + hh:stat0 + N_HEADS + hh + 1] = l[sl]
        yield


def _attn_plan(n_rows, seq_len, dil, n_res, to_bcl):
    sub_len = seq_len // dil
    tl = min(SSD_STEP_ROWS, sub_len)
    n_lblocks = sub_len // tl
    r = BAND_RADIUS
    halo_per_block = tl // r
    n_halo = n_rows // r
    width = n_res * D_ATTN

    def main(*ids):
        b, c, l = to_bcl(*ids)
        return (b * n_lblocks + l, c)

    def prev(*ids):
        b, c, l = to_bcl(*ids)
        return (jnp.maximum((b * n_lblocks + l) * halo_per_block - 1, 0), c)

    def nxt(*ids):
        b, c, l = to_bcl(*ids)
        return (jnp.minimum((b * n_lblocks + l + 1) * halo_per_block, n_halo - 1), c)

    blk = pl.BlockSpec((tl, width), main)
    hp = pl.BlockSpec((r, width), prev)
    hn = pl.BlockSpec((r, width), nxt)
    return dict(
        tl=tl, n_lblocks=n_lblocks,
        in_specs=[blk, blk, hp, hn, blk, hp, hn],
        out_specs=[blk, pl.BlockSpec((tl, n_res * LANES), main)],
        out_shape=[jax.ShapeDtypeStruct((n_rows, dil * D_ATTN), BF16),
                   jax.ShapeDtypeStruct((n_rows, dil * LANES), F32)],
        scratch=[pltpu.VMEM((3, 2 * Q_SUB, K_WIN), F32)])


def _out_proj_kernel(x_ref, ys_ref, o1_ref, o2_ref, o3_ref, l1_ref, l2_ref, l3_ref,
                     za_ref, wo_ref, gf_ref, y_ref, o_scr, omid_scr, l_scr, lmid_scr):
    tm = x_ref.shape[0]

    def token_order(src_ref, scr, mid, dil, width):
        for j in range(width // LANES):
            if dil == SUB_DIL:
                for c in range(dil):
                    col = c * width + j * LANES
                    scr[j, pl.ds(c, tm // dil, stride=dil), :] = (
                        src_ref[:, col:col + LANES].astype(F32))
            else:
                for c4 in range(SUB_DIL):
                    for c2 in range(SUB_DIL):
                        col = (c4 + SUB_DIL * c2) * width + j * LANES
                        mid[j, c4, pl.ds(c2, tm // dil, stride=SUB_DIL), :] = (
                            src_ref[:, col:col + LANES].astype(F32))
                for c4 in range(SUB_DIL):
                    scr[j, pl.ds(c4, tm // SUB_DIL, stride=SUB_DIL), :] = mid[j, c4]
        return jnp.concatenate([scr[j] for j in range(width // LANES)], axis=1)

    s1 = l1_ref[...]
    s2 = token_order(l2_ref, l_scr.at[0], lmid_scr, DILATIONS[1], LANES)
    s3 = token_order(l3_ref, l_scr.at[1], lmid_scr, DILATIONS[2], LANES)
    m = jnp.maximum(jnp.maximum(s1, s2), s3)
    head_lane = lax.broadcasted_iota(jnp.int32, s1.shape, 1) < N_HEADS
    a = [jnp.exp2(s - m) for s in (s1, s2, s3)]
    den = sum(ap * pltpu.roll(s, LANES - N_HEADS, 1) for ap, s in zip(a, (s1, s2, s3)))
    inv = jnp.where(head_lane, 1.0 / jnp.where(head_lane, den, 1.0), 0.0)
    lane = lax.broadcasted_iota(jnp.int32, (LANES, D_ATTN), 1)
    rowi = lax.broadcasted_iota(jnp.int32, (LANES, D_ATTN), 0)
    expand = jnp.where(rowi == lax.shift_right_logical(lane, HEAD_SHIFT), 1.0, 0.0).astype(BF16)

    token_order(o2_ref, o_scr.at[0], omid_scr, DILATIONS[1], D_ATTN)
    token_order(o3_ref, o_scr.at[1], omid_scr, DILATIONS[2], D_ATTN)
    w_small = [(ap * inv).astype(BF16) for ap in a]
    out = x_ref[...]
    out = out + jnp.dot(ys_ref[...], wo_ref[:D_SSD, :], preferred_element_type=F32)
    group = 2 * LANES
    for g in range(D_ATTN // group):
        cols = slice(g * group, (g + 1) * group)
        blocks = range(g * group // LANES, (g + 1) * group // LANES)

        def weight(wp):
            return jnp.dot(wp, expand[:, cols], preferred_element_type=F32)

        mix = weight(w_small[0]) * o1_ref[:, cols].astype(F32)
        for p in (1, 2):
            o_tok = jnp.concatenate([o_scr[p - 1, j] for j in blocks], axis=1)
            mix = mix + weight(w_small[p]) * o_tok
        y_attn = (mix * _silu(za_ref[:, cols].astype(F32))).astype(BF16)
        out = out + jnp.dot(y_attn, wo_ref[D_SSD + g * group:D_SSD + (g + 1) * group, :],
                            preferred_element_type=F32)
    ms = jnp.mean(out * out, axis=-1, keepdims=True)
    y_ref[...] = out * lax.rsqrt(ms + EPS) * gf_ref[...]


def _out_proj(x2, y_ssd, os_, lses, za, w_out, norm_f, tm):
    t_total = x2.shape[0]
    row = lambda i: (i, 0)
    wide = pl.BlockSpec((tm, D_MODEL), row)
    o_specs = [pl.BlockSpec((tm // d, d * D_ATTN), row) for d in DILATIONS]
    l_specs = [pl.BlockSpec((tm // d, d * LANES), row) for d in DILATIONS]
    return pl.pallas_call(
        _out_proj_kernel,
        grid=(t_total // tm,),
        in_specs=[wide, wide] + o_specs + l_specs
                 + [wide, _const_spec(w_out.shape), _const_spec(norm_f.shape)],
        out_specs=wide,
        out_shape=jax.ShapeDtypeStruct((t_total, D_MODEL), F32),
        scratch_shapes=[pltpu.VMEM((2, D_ATTN // LANES, tm, LANES), F32),
                        pltpu.VMEM((D_ATTN // LANES, SUB_DIL, tm // SUB_DIL, LANES), F32),
                        pltpu.VMEM((2, 1, tm, LANES), F32),
                        pltpu.VMEM((1, SUB_DIL, tm // SUB_DIL, LANES), F32)],
        compiler_params=_cparams(("arbitrary",)),
        name="out_proj",
    )(x2, y_ssd, *os_, *lses, za, w_out, norm_f)


def _rope_tables(seq_len):
    half = HEAD_DIM // 2
    inv = ROPE_THETA ** (-jnp.arange(half, dtype=F32) * 2.0 / HEAD_DIM)
    ang = jnp.arange(seq_len, dtype=F32)[:, None] * inv[None, :]
    cos, sin = jnp.cos(ang), jnp.sin(ang)
    cos_h = jnp.concatenate([cos, cos], axis=-1)
    sin_h = jnp.concatenate([-sin, sin], axis=-1)
    reps = LANES // HEAD_DIM
    return jnp.tile(cos_h, (1, reps)), jnp.tile(sin_h, (1, reps))


def _pad_lanes(v, offset=0):
    out = jnp.zeros((1, LANES), F32)
    return out.at[0, offset:offset + v.shape[0]].set(v.astype(F32))


def _encoder(x, tables, params):
    (ws, conv_w, conv_b, dtb, alog, dskip_x, norm_ssd, norm_in, w_out, norm_f) = params
    batch, seq_len, _ = x.shape
    t_total = batch * seq_len
    x2 = x.reshape(t_total, D_MODEL)
    cos, sin = tables
    q, k, v, za, zs, xbc, dt = _in_proj(x2, norm_in, cos[:seq_len], sin[:seq_len], ws,
                                        seq_len, tm=IN_PROJ_ROWS)
    fwd_plan = _ssd_fwd_plan(t_total, seq_len,
                             (conv_w.shape, conv_b.shape, dtb.shape, alog.shape))
    (y_f, act), (attn4,) = _ssd_with_attn(
        fwd_plan, (xbc, xbc, xbc, dt, conv_w, conv_b, dtb, alog), False,
        [(q[1], k[1], v[1], DILATIONS[1])], batch, seq_len)
    bwd_plan = _ssd_bwd_plan(t_total, seq_len,
                             (dtb.shape, alog.shape, dskip_x.shape, norm_ssd.shape))
    (y_ssd,), (attn1, attn16) = _ssd_with_attn(
        bwd_plan, (act, dt, dtb, alog, y_f, zs, dskip_x, norm_ssd), True,
        [(q[0], k[0], v[0], DILATIONS[0]), (q[2], k[2], v[2], DILATIONS[2])], batch, seq_len)
    os_, stats = zip(attn1, attn4, attn16)
    y = _out_proj(x2, y_ssd, os_, stats, za, w_out, norm_f, tm=OUT_PROJ_ROWS)
    return y.reshape(batch, seq_len, D_MODEL)


def kernel(x_prompt, x_sample, w_in, conv_w, conv_b, dt_bias_fwd, dt_bias_bwd, a_log_fwd,
           a_log_bwd, d_skip, norm_ssd, norm_in, w_out, norm_f):
    assert w_in.shape[0] == 1, "single-layer encoder"
    w = w_in[0]
    bounds = [0, D_ATTN, 2 * D_ATTN, 3 * D_ATTN, 4 * D_ATTN, 4 * D_ATTN + D_SSD,
              4 * D_ATTN + D_SSD + D_XBC]
    segs = [w[:, bounds[i]:bounds[i + 1]].astype(BF16) for i in range(6)]
    w_dt = jnp.zeros((D_MODEL, LANES), F32).at[:, :2 * N_HEADS].set(w[:, bounds[6]:])
    ws = tuple(segs) + (w_dt.astype(BF16),)
    dtb = _pad_lanes(dt_bias_fwd[0]) + _pad_lanes(dt_bias_bwd[0], N_HEADS)
    alog = _pad_lanes(a_log_fwd[0]) + _pad_lanes(a_log_bwd[0], N_HEADS)
    dskip_x = jnp.repeat(d_skip[0].astype(F32), HEAD_DIM)[None, :]
    params = (ws, conv_w[0].astype(F32), conv_b[0].astype(F32)[None, :], dtb, alog, dskip_x,
              norm_ssd[0].astype(F32)[None, :], norm_in[0].astype(F32)[None, :],
              w_out[0].astype(BF16), norm_f.astype(F32)[None, :])
    tables = _rope_tables(max(x_prompt.shape[1], x_sample.shape[1]))
    return (_encoder(x_prompt, tables, params), _encoder(x_sample, tables, params))
```
